```python
import math
import jax, jax.numpy as jnp
from jax import lax
import numpy as np

D_MODEL = 1024
BATCH = 16
SEQ = 2048
DEPTH = 1

N_META = 16
D_MIX = 2 * D_MODEL
SSD_HEAD_DIM = 64
SSD_HEADS = 24
D_SSD = SSD_HEADS * SSD_HEAD_DIM
SSD_GROUPS = 4
SSD_STATE = 128
SSD_CONV = 4
SSD_CHUNK = 128
D_XBC = D_SSD + 2 * SSD_GROUPS * SSD_STATE
D_SC = D_MIX - D_SSD
SC_CONV = 3
D_IN = D_SSD + D_XBC + SSD_HEADS + 3 * D_SC
PEER_HEADS = 8
PEER_KEYS = 128
PEER_EXPERTS = PEER_KEYS * PEER_KEYS
PEER_TOPK = 16
PEER_QDIM = 256
PEER_HALF = PEER_QDIM // 2
PEER_BLOCK = 512
EPS = 1e-6

kernel_name = "hymba_ssd_shortconv_peer_layer"


def rmsnorm(x, g):
    xf = x.astype(jnp.float32)
    xf = xf * lax.rsqrt(jnp.mean(xf * xf, axis=-1, keepdims=True) + EPS)
    return (xf * g.astype(jnp.float32)).astype(x.dtype)


def causal_dwconv(x, w):
    width, ch = w.shape
    return lax.conv_general_dilated(
        x, w[:, None, :].astype(x.dtype), window_strides=(1,), padding=[(width - 1, 0)],
        dimension_numbers=('NWC', 'WIO', 'NWC'), feature_group_count=ch)


def segsum(a):
    t = a.shape[-1]
    ar = jnp.broadcast_to(a[..., None], a.shape + (t,))
    ar = jnp.where(jnp.tril(jnp.ones((t, t), bool), -1), ar, 0.0)
    cs = jnp.cumsum(ar, axis=-2)
    return jnp.where(jnp.tril(jnp.ones((t, t), bool), 0), cs, -jnp.inf)


def ssd_scan(x, dt, A, Bm, Cm):
    b, l, h, p = x.shape
    g, n = Bm.shape[2], Bm.shape[3]
    r, q = h // g, SSD_CHUNK
    c = l // q
    xc = x.reshape(b, c, q, g, r, p)
    Bc = Bm.reshape(b, c, q, g, n)
    Cc = Cm.reshape(b, c, q, g, n)
    dtc = dt.reshape(b, c, q, g, r)
    a = (dtc * A.reshape(g, r)).transpose(0, 3, 4, 1, 2)
    x_dt = xc * dtc[..., None]
    a_cs = jnp.cumsum(a, axis=-1)
    decay_in = jnp.exp(segsum(a))
    cb = jnp.einsum('bclgn,bcsgn->bgcls', Cc, Bc)
    y_diag = jnp.einsum('bgcls,bgrcls,bcsgrp->bclgrp', cb, decay_in, x_dt)
    decay_states = jnp.exp(a_cs[..., -1:] - a_cs)
    states = jnp.einsum('bcsgn,bgrcs,bcsgrp->bcgrpn', Bc, decay_states, x_dt)
    states = jnp.concatenate([jnp.zeros_like(states[:, :1]), states], axis=1)
    chunk_a = jnp.pad(a_cs[..., -1], ((0, 0), (0, 0), (0, 0), (1, 0)))
    decay_chunk = jnp.exp(segsum(chunk_a))
    new_states = jnp.einsum('bgrzc,bcgrpn->bzgrpn', decay_chunk, states)
    prev_states = new_states[:, :-1]
    y_off = jnp.einsum('bclgn,bcgrpn,bgrcl->bclgrp', Cc, prev_states, jnp.exp(a_cs))
    return (y_diag + y_off).reshape(b, l, h, p)


def hybrid_mixer(n, w_in, conv_ssd_w, conv_ssd_b, dt_bias, a_log, d_skip, ssd_norm_w,
                 conv_sc_w, w_out):
    bsz, L, _ = n.shape
    proj = n @ w_in.astype(n.dtype)
    s1 = D_SSD
    s2 = s1 + D_XBC
    s3 = s2 + SSD_HEADS
    s4 = s3 + D_SC
    s5 = s4 + D_SC
    z, xbc, dt_raw, sc_b, sc_c, sc_x = jnp.split(proj, [s1, s2, s3, s4, s5], axis=-1)

    xbc = jax.nn.silu(causal_dwconv(xbc, conv_ssd_w) + conv_ssd_b.astype(n.dtype))
    xs, Bm, Cm = jnp.split(xbc, [D_SSD, D_SSD + SSD_GROUPS * SSD_STATE], axis=-1)
    xs_h = xs.reshape(bsz, L, SSD_HEADS, SSD_HEAD_DIM)
    dt = jax.nn.softplus(dt_raw.astype(jnp.float32) + dt_bias.astype(jnp.float32))
    A = -jnp.exp(a_log.astype(jnp.float32))
    pad_len = SSD_CHUNK - N_META

    def lpad(t):
        return jnp.pad(t, ((0, 0), (pad_len, 0)) + ((0, 0),) * (t.ndim - 2))

    y = ssd_scan(lpad(xs_h.astype(jnp.float32)), lpad(dt), A,
                 lpad(Bm.reshape(bsz, L, SSD_GROUPS, SSD_STATE).astype(jnp.float32)),
                 lpad(Cm.reshape(bsz, L, SSD_GROUPS, SSD_STATE).astype(jnp.float32)))
    y = y[:, pad_len:] + d_skip.astype(jnp.float32)[:, None] * xs_h.astype(jnp.float32)
    y = y.reshape(bsz, L, D_SSD) * jax.nn.silu(z.astype(jnp.float32))
    y_ssd = rmsnorm(y.reshape(bsz, L, SSD_GROUPS, D_SSD // SSD_GROUPS),
                    ssd_norm_w.reshape(SSD_GROUPS, D_SSD // SSD_GROUPS))
    y_ssd = y_ssd.reshape(bsz, L, D_SSD).astype(n.dtype)

    y_sc = sc_b * causal_dwconv(sc_c * sc_x, conv_sc_w)

    return jnp.concatenate([y_ssd, y_sc], axis=-1) @ w_out.astype(n.dtype)


def peer_ffn(n, w_q, sub_keys, expert_u, expert_v):
    bsz, L, d = n.shape
    t = n.reshape(-1, d)
    T = t.shape[0]
    q = (t @ w_q.astype(n.dtype)).reshape(T, PEER_HEADS, 2, PEER_HALF)
    s = jnp.einsum('thsd,hskd->thsk', q, sub_keys.astype(n.dtype)).astype(jnp.float32)
    v_half, i_half = lax.top_k(s, PEER_TOPK)
    cand = (v_half[:, :, 0, :, None] + v_half[:, :, 1, None, :]).reshape(T, PEER_HEADS, PEER_TOPK * PEER_TOPK)
    best, pos = lax.top_k(cand, PEER_TOPK)
    i1 = jnp.take_along_axis(i_half[:, :, 0], pos // PEER_TOPK, axis=-1)
    i2 = jnp.take_along_axis(i_half[:, :, 1], pos % PEER_TOPK, axis=-1)
    experts = (i1 * PEER_KEYS + i2).reshape(T, PEER_HEADS * PEER_TOPK)
    gates = jax.nn.softmax(best, axis=-1).reshape(T, PEER_HEADS * PEER_TOPK).astype(n.dtype)

    nb = -(-T // PEER_BLOCK)
    padT = nb * PEER_BLOCK - T
    tp = jnp.pad(t, ((0, padT), (0, 0))).reshape(nb, PEER_BLOCK, d)
    ep = jnp.pad(experts, ((0, padT), (0, 0))).reshape(nb, PEER_BLOCK, -1)
    gp = jnp.pad(gates, ((0, padT), (0, 0))).reshape(nb, PEER_BLOCK, -1)

    def block(args):
        xb, eb, gb = args
        u = expert_u[eb].astype(xb.dtype)
        act = jax.nn.gelu(jnp.einsum('td,tkd->tk', xb, u), approximate=False) * gb
        return jnp.einsum('tk,tkd->td', act, expert_v[eb].astype(xb.dtype))

    out = lax.map(block, (tp, ep, gp))
    return out.reshape(-1, d)[:T].reshape(bsz, L, d)


def setup_inputs(seed: int = 0) -> dict:
    key = jax.random.key(seed)
    ks = jax.random.split(key, 20)

    def nrm(k, shape, scale):
        return jax.random.normal(k, shape, jnp.float32) * scale

    x = nrm(ks[0], (BATCH, SEQ, D_MODEL), 1.0)
    meta_tokens = nrm(ks[1], (N_META, D_MODEL), 1.0)
    g_mix = 1.0 + nrm(ks[2], (DEPTH, D_MODEL), 0.05)
    w_in = nrm(ks[3], (DEPTH, D_MODEL, D_IN), D_MODEL ** -0.5)
    conv_ssd_w = nrm(ks[4], (DEPTH, SSD_CONV, D_XBC), SSD_CONV ** -0.5)
    conv_ssd_b = nrm(ks[5], (DEPTH, D_XBC), 0.02)
    dt0 = jnp.exp(jax.random.uniform(ks[6], (DEPTH, SSD_HEADS), jnp.float32,
                                     minval=math.log(1e-3), maxval=math.log(1e-1)))
    dt_bias = dt0 + jnp.log(-jnp.expm1(-dt0))
    a_log = jnp.log(jax.random.uniform(ks[7], (DEPTH, SSD_HEADS), jnp.float32, minval=1.0, maxval=16.0))
    d_skip = 1.0 + nrm(ks[8], (DEPTH, SSD_HEADS), 0.05)
    ssd_norm_w = 1.0 + nrm(ks[9], (DEPTH, D_SSD), 0.05)
    conv_sc_w = nrm(ks[10], (DEPTH, SC_CONV, D_SC), SC_CONV ** -0.5)
    w_out = nrm(ks[11], (DEPTH, D_MIX, D_MODEL), D_MIX ** -0.5)
    g_ffn = 1.0 + nrm(ks[12], (DEPTH, D_MODEL), 0.05)
    w_q = nrm(ks[13], (DEPTH, D_MODEL, PEER_HEADS * PEER_QDIM), D_MODEL ** -0.5)
    sub_keys = nrm(ks[14], (DEPTH, PEER_HEADS, 2, PEER_KEYS, PEER_HALF), PEER_HALF ** -0.5)
    expert_u = nrm(ks[15], (DEPTH, PEER_EXPERTS, D_MODEL), D_MODEL ** -0.5)
    expert_v = nrm(ks[16], (DEPTH, PEER_EXPERTS, D_MODEL), PEER_HEADS ** -0.5)
    g_final = 1.0 + nrm(ks[17], (D_MODEL,), 0.05)
    return {"x": x, "meta_tokens": meta_tokens, "g_mix": g_mix, "w_in": w_in,
            "conv_ssd_w": conv_ssd_w, "conv_ssd_b": conv_ssd_b, "dt_bias": dt_bias,
            "a_log": a_log, "d_skip": d_skip, "ssd_norm_w": ssd_norm_w,
            "conv_sc_w": conv_sc_w, "w_out": w_out, "g_ffn": g_ffn, "w_q": w_q,
            "sub_keys": sub_keys, "expert_u": expert_u, "expert_v": expert_v,
            "g_final": g_final}


def reference(x, meta_tokens, g_mix, w_in, conv_ssd_w, conv_ssd_b, dt_bias, a_log, d_skip,
              ssd_norm_w, conv_sc_w, w_out, g_ffn, w_q, sub_keys, expert_u, expert_v, g_final):
    bsz = x.shape[0]
    meta = jnp.broadcast_to(meta_tokens[None].astype(x.dtype), (bsz, N_META, D_MODEL))
    h = jnp.concatenate([meta, x], axis=1)
    for l in range(DEPTH):
        h = h + hybrid_mixer(rmsnorm(h, g_mix[l]), w_in[l], conv_ssd_w[l], conv_ssd_b[l],
                             dt_bias[l], a_log[l], d_skip[l], ssd_norm_w[l], conv_sc_w[l], w_out[l])
        h = h + peer_ffn(rmsnorm(h, g_ffn[l]), w_q[l], sub_keys[l], expert_u[l], expert_v[l])
    h = rmsnorm(h, g_final)
    return h[:, N_META:]
```

```python
import functools
import math

import jax
import jax.numpy as jnp
from jax import lax
from jax.experimental import pallas as pl
from jax.experimental.pallas import tpu as pltpu

D_MODEL = 1024
N_META = 16
D_MIX = 2 * D_MODEL
SSD_HEAD_DIM = 64
SSD_HEADS = 24
D_SSD = SSD_HEADS * SSD_HEAD_DIM
SSD_GROUPS = 4
SSD_STATE = 128
SSD_CONV = 4
SSD_CHUNK = 128
D_XBC = D_SSD + 2 * SSD_GROUPS * SSD_STATE
D_SC = D_MIX - D_SSD
SC_CONV = 3
D_IN = D_SSD + D_XBC + SSD_HEADS + 3 * D_SC
PEER_HEADS = 8
PEER_KEYS = 128
PEER_TOPK = 16
PEER_QDIM = 256
PEER_HALF = PEER_QDIM // 2
PEER_BLOCK = 512
EPS = 1e-6


def _rmsnorm(x, g):
    xf = x.astype(jnp.float32)
    xf = xf * lax.rsqrt(jnp.mean(xf * xf, axis=-1, keepdims=True) + EPS)
    return (xf * g.astype(jnp.float32)).astype(x.dtype)


def _causal_dwconv(x, w):
    width, ch = w.shape
    return lax.conv_general_dilated(
        x, w[:, None, :].astype(x.dtype), window_strides=(1,), padding=[(width - 1, 0)],
        dimension_numbers=('NWC', 'WIO', 'NWC'), feature_group_count=ch)


def _segsum(a):
    t = a.shape[-1]
    ar = jnp.broadcast_to(a[..., None], a.shape + (t,))
    ar = jnp.where(jnp.tril(jnp.ones((t, t), bool), -1), ar, 0.0)
    cs = jnp.cumsum(ar, axis=-2)
    return jnp.where(jnp.tril(jnp.ones((t, t), bool), 0), cs, -jnp.inf)


def _ssd_scan(x, dt, A, Bm, Cm):
    b, l, h, p = x.shape
    g, n = Bm.shape[2], Bm.shape[3]
    r, q = h // g, SSD_CHUNK
    c = l // q
    xc = x.reshape(b, c, q, g, r, p)
    Bc = Bm.reshape(b, c, q, g, n)
    Cc = Cm.reshape(b, c, q, g, n)
    dtc = dt.reshape(b, c, q, g, r)
    a = (dtc * A.reshape(g, r)).transpose(0, 3, 4, 1, 2)
    x_dt = xc * dtc[..., None]
    a_cs = jnp.cumsum(a, axis=-1)
    decay_in = jnp.exp(_segsum(a))
    cb = jnp.einsum('bclgn,bcsgn->bgcls', Cc, Bc)
    y_diag = jnp.einsum('bgcls,bgrcls,bcsgrp->bclgrp', cb, decay_in, x_dt)
    decay_states = jnp.exp(a_cs[..., -1:] - a_cs)
    states = jnp.einsum('bcsgn,bgrcs,bcsgrp->bcgrpn', Bc, decay_states, x_dt)
    states = jnp.concatenate([jnp.zeros_like(states[:, :1]), states], axis=1)
    chunk_a = jnp.pad(a_cs[..., -1], ((0, 0), (0, 0), (0, 0), (1, 0)))
    decay_chunk = jnp.exp(_segsum(chunk_a))
    new_states = jnp.einsum('bgrzc,bcgrpn->bzgrpn', decay_chunk, states)
    prev_states = new_states[:, :-1]
    y_off = jnp.einsum('bclgn,bcgrpn,bgrcl->bclgrp', Cc, prev_states, jnp.exp(a_cs))
    return (y_diag + y_off).reshape(b, l, h, p)


def _hybrid_mixer(n, w_in, conv_ssd_w, conv_ssd_b, dt_bias, a_log, d_skip, ssd_norm_w,
                  conv_sc_w, w_out):
    bsz, L, _ = n.shape
    proj = n @ w_in.astype(n.dtype)
    s1 = D_SSD
    s2 = s1 + D_XBC
    s3 = s2 + SSD_HEADS
    s4 = s3 + D_SC
    s5 = s4 + D_SC
    z, xbc, dt_raw, sc_b, sc_c, sc_x = jnp.split(proj, [s1, s2, s3, s4, s5], axis=-1)
    xbc = jax.nn.silu(_causal_dwconv(xbc, conv_ssd_w) + conv_ssd_b.astype(n.dtype))
    xs, Bm, Cm = jnp.split(xbc, [D_SSD, D_SSD + SSD_GROUPS * SSD_STATE], axis=-1)
    xs_h = xs.reshape(bsz, L, SSD_HEADS, SSD_HEAD_DIM)
    dt = jax.nn.softplus(dt_raw.astype(jnp.float32) + dt_bias.astype(jnp.float32))
    A = -jnp.exp(a_log.astype(jnp.float32))
    pad_len = SSD_CHUNK - N_META

    def lpad(t):
        return jnp.pad(t, ((0, 0), (pad_len, 0)) + ((0, 0),) * (t.ndim - 2))

    y = _ssd_scan(lpad(xs_h.astype(jnp.float32)), lpad(dt), A,
                  lpad(Bm.reshape(bsz, L, SSD_GROUPS, SSD_STATE).astype(jnp.float32)),
                  lpad(Cm.reshape(bsz, L, SSD_GROUPS, SSD_STATE).astype(jnp.float32)))
    y = y[:, pad_len:] + d_skip.astype(jnp.float32)[:, None] * xs_h.astype(jnp.float32)
    y = y.reshape(bsz, L, D_SSD) * jax.nn.silu(z.astype(jnp.float32))
    y_ssd = _rmsnorm(y.reshape(bsz, L, SSD_GROUPS, D_SSD // SSD_GROUPS),
                     ssd_norm_w.reshape(SSD_GROUPS, D_SSD // SSD_GROUPS))
    y_ssd = y_ssd.reshape(bsz, L, D_SSD).astype(n.dtype)
    y_sc = sc_b * _causal_dwconv(sc_c * sc_x, conv_sc_w)
    return jnp.concatenate([y_ssd, y_sc], axis=-1) @ w_out.astype(n.dtype)


def _peer_ffn(n, w_q, sub_keys, expert_u, expert_v):
    bsz, L, d = n.shape
    t = n.reshape(-1, d)
    T = t.shape[0]
    q = (t @ w_q.astype(n.dtype)).reshape(T, PEER_HEADS, 2, PEER_HALF)
    s = jnp.einsum('thsd,hskd->thsk', q, sub_keys.astype(n.dtype)).astype(jnp.float32)
    v_half, i_half = lax.top_k(s, PEER_TOPK)
    cand = (v_half[:, :, 0, :, None] + v_half[:, :, 1, None, :]).reshape(
        T, PEER_HEADS, PEER_TOPK * PEER_TOPK)
    best, pos = lax.top_k(cand, PEER_TOPK)
    i1 = jnp.take_along_axis(i_half[:, :, 0], pos // PEER_TOPK, axis=-1)
    i2 = jnp.take_along_axis(i_half[:, :, 1], pos % PEER_TOPK, axis=-1)
    experts = (i1 * PEER_KEYS + i2).reshape(T, PEER_HEADS * PEER_TOPK)
    gates = jax.nn.softmax(best, axis=-1).reshape(T, PEER_HEADS * PEER_TOPK).astype(n.dtype)

    nb = -(-T // PEER_BLOCK)
    padT = nb * PEER_BLOCK - T
    tp = jnp.pad(t, ((0, padT), (0, 0))).reshape(nb, PEER_BLOCK, d)
    ep = jnp.pad(experts, ((0, padT), (0, 0))).reshape(nb, PEER_BLOCK, -1)
    gp = jnp.pad(gates, ((0, padT), (0, 0))).reshape(nb, PEER_BLOCK, -1)

    def block(args):
        xb, eb, gb = args
        u = expert_u[eb].astype(xb.dtype)
        act = jax.nn.gelu(jnp.einsum('td,tkd->tk', xb, u), approximate=False) * gb
        return jnp.einsum('tk,tkd->td', act, expert_v[eb].astype(xb.dtype))

    out = lax.map(block, (tp, ep, gp))
    return out.reshape(-1, d)[:T].reshape(bsz, L, d)


def _final_norm_body(h_ref, g_ref, o_ref):
    xf = h_ref[...]
    ms = jnp.mean(xf * xf, axis=-1, keepdims=True)
    o_ref[...] = xf * lax.rsqrt(ms + EPS) * g_ref[...]


def _final_norm(h, g):
    rows, d = h.shape
    tile = 512
    return pl.pallas_call(
        _final_norm_body,
        grid=(rows // tile,),
        in_specs=[pl.BlockSpec((tile, d), lambda i: (i, 0)),
                  pl.BlockSpec((1, d), lambda i: (0, 0))],
        out_specs=pl.BlockSpec((tile, d), lambda i: (i, 0)),
        out_shape=jax.ShapeDtypeStruct((rows, d), jnp.float32),
        name="final_norm",
    )(h, g.reshape(1, d))


def kernel(x, meta_tokens, g_mix, w_in, conv_ssd_w, conv_ssd_b, dt_bias, a_log, d_skip,
           ssd_norm_w, conv_sc_w, w_out, g_ffn, w_q, sub_keys, expert_u, expert_v, g_final):
    bsz, seq, d = x.shape
    meta = jnp.broadcast_to(meta_tokens[None].astype(x.dtype), (bsz, N_META, d))
    h = jnp.concatenate([meta, x], axis=1)
    for l in range(g_mix.shape[0]):
        h = h + _hybrid_mixer(_rmsnorm(h, g_mix[l]), w_in[l], conv_ssd_w[l], conv_ssd_b[l],
                              dt_bias[l], a_log[l], d_skip[l], ssd_norm_w[l], conv_sc_w[l],
                              w_out[l])
        h = h + _peer_ffn(_rmsnorm(h, g_ffn[l]), w_q[l], sub_keys[l], expert_u[l], expert_v[l])
    out = _final_norm(h[:, N_META:].reshape(bsz * seq, d), g_final)
    return out.reshape(bsz, seq, d)
```

```python
import functools
import math

import jax
import jax.numpy as jnp
from jax import lax
from jax.experimental import pallas as pl
from jax.experimental.pallas import tpu as pltpu

D_MODEL = 1024
N_META = 16
D_MIX = 2 * D_MODEL
SSD_HEAD_DIM = 64
SSD_HEADS = 24
D_SSD = SSD_HEADS * SSD_HEAD_DIM
SSD_GROUPS = 4
SSD_STATE = 128
SSD_CONV = 4
SSD_CHUNK = 128
D_XBC = D_SSD + 2 * SSD_GROUPS * SSD_STATE
D_SC = D_MIX - D_SSD
SC_CONV = 3
D_IN = D_SSD + D_XBC + SSD_HEADS + 3 * D_SC
PEER_HEADS = 8
PEER_KEYS = 128
PEER_TOPK = 16
PEER_QDIM = 256
PEER_HALF = PEER_QDIM // 2
PEER_BLOCK = 512
EPS = 1e-6


def _rmsnorm(x, g):
    xf = x.astype(jnp.float32)
    xf = xf * lax.rsqrt(jnp.mean(xf * xf, axis=-1, keepdims=True) + EPS)
    return (xf * g.astype(jnp.float32)).astype(x.dtype)


def _causal_dwconv(x, w):
    width, ch = w.shape
    return lax.conv_general_dilated(
        x, w[:, None, :].astype(x.dtype), window_strides=(1,), padding=[(width - 1, 0)],
        dimension_numbers=('NWC', 'WIO', 'NWC'), feature_group_count=ch)


def _segsum(a):
    t = a.shape[-1]
    ar = jnp.broadcast_to(a[..., None], a.shape + (t,))
    ar = jnp.where(jnp.tril(jnp.ones((t, t), bool), -1), ar, 0.0)
    cs = jnp.cumsum(ar, axis=-2)
    return jnp.where(jnp.tril(jnp.ones((t, t), bool), 0), cs, -jnp.inf)


def _ssd_scan(x, dt, A, Bm, Cm):
    b, l, h, p = x.shape
    g, n = Bm.shape[2], Bm.shape[3]
    r, q = h // g, SSD_CHUNK
    c = l // q
    xc = x.reshape(b, c, q, g, r, p)
    Bc = Bm.reshape(b, c, q, g, n)
    Cc = Cm.reshape(b, c, q, g, n)
    dtc = dt.reshape(b, c, q, g, r)
    a = (dtc * A.reshape(g, r)).transpose(0, 3, 4, 1, 2)
    x_dt = xc * dtc[..., None]
    a_cs = jnp.cumsum(a, axis=-1)
    decay_in = jnp.exp(_segsum(a))
    cb = jnp.einsum('bclgn,bcsgn->bgcls', Cc, Bc)
    y_diag = jnp.einsum('bgcls,bgrcls,bcsgrp->bclgrp', cb, decay_in, x_dt)
    decay_states = jnp.exp(a_cs[..., -1:] - a_cs)
    states = jnp.einsum('bcsgn,bgrcs,bcsgrp->bcgrpn', Bc, decay_states, x_dt)
    states = jnp.concatenate([jnp.zeros_like(states[:, :1]), states], axis=1)
    chunk_a = jnp.pad(a_cs[..., -1], ((0, 0), (0, 0), (0, 0), (1, 0)))
    decay_chunk = jnp.exp(_segsum(chunk_a))
    new_states = jnp.einsum('bgrzc,bcgrpn->bzgrpn', decay_chunk, states)
    prev_states = new_states[:, :-1]
    y_off = jnp.einsum('bclgn,bcgrpn,bgrcl->bclgrp', Cc, prev_states, jnp.exp(a_cs))
    return (y_diag + y_off).reshape(b, l, h, p)


def _hybrid_mixer(n, w_in, conv_ssd_w, conv_ssd_b, dt_bias, a_log, d_skip, ssd_norm_w,
                  conv_sc_w, w_out):
    bsz, L, _ = n.shape
    proj = n @ w_in.astype(n.dtype)
    s1 = D_SSD
    s2 = s1 + D_XBC
    s3 = s2 + SSD_HEADS
    s4 = s3 + D_SC
    s5 = s4 + D_SC
    z, xbc, dt_raw, sc_b, sc_c, sc_x = jnp.split(proj, [s1, s2, s3, s4, s5], axis=-1)
    xbc = jax.nn.silu(_causal_dwconv(xbc, conv_ssd_w) + conv_ssd_b.astype(n.dtype))
    xs, Bm, Cm = jnp.split(xbc, [D_SSD, D_SSD + SSD_GROUPS * SSD_STATE], axis=-1)
    xs_h = xs.reshape(bsz, L, SSD_HEADS, SSD_HEAD_DIM)
    dt = jax.nn.softplus(dt_raw.astype(jnp.float32) + dt_bias.astype(jnp.float32))
    A = -jnp.exp(a_log.astype(jnp.float32))
    pad_len = SSD_CHUNK - N_META

    def lpad(t):
        return jnp.pad(t, ((0, 0), (pad_len, 0)) + ((0, 0),) * (t.ndim - 2))

    y = _ssd_scan(lpad(xs_h.astype(jnp.float32)), lpad(dt), A,
                  lpad(Bm.reshape(bsz, L, SSD_GROUPS, SSD_STATE).astype(jnp.float32)),
                  lpad(Cm.reshape(bsz, L, SSD_GROUPS, SSD_STATE).astype(jnp.float32)))
    y = y[:, pad_len:] + d_skip.astype(jnp.float32)[:, None] * xs_h.astype(jnp.float32)
    y = y.reshape(bsz, L, D_SSD) * jax.nn.silu(z.astype(jnp.float32))
    y_ssd = _rmsnorm(y.reshape(bsz, L, SSD_GROUPS, D_SSD // SSD_GROUPS),
                     ssd_norm_w.reshape(SSD_GROUPS, D_SSD // SSD_GROUPS))
    y_ssd = y_ssd.reshape(bsz, L, D_SSD).astype(n.dtype)
    y_sc = sc_b * _causal_dwconv(sc_c * sc_x, conv_sc_w)
    return jnp.concatenate([y_ssd, y_sc], axis=-1) @ w_out.astype(n.dtype)


def _peer_ffn(n, w_q, sub_keys, expert_u, expert_v):
    bsz, L, d = n.shape
    t = n.reshape(-1, d)
    T = t.shape[0]
    q = (t @ w_q.astype(n.dtype)).reshape(T, PEER_HEADS, 2, PEER_HALF)
    s = jnp.einsum('thsd,hskd->thsk', q, sub_keys.astype(n.dtype)).astype(jnp.float32)
    v_half, i_half = lax.top_k(s, PEER_TOPK)
    cand = (v_half[:, :, 0, :, None] + v_half[:, :, 1, None, :]).reshape(
        T, PEER_HEADS, PEER_TOPK * PEER_TOPK)
    best, pos = lax.top_k(cand, PEER_TOPK)
    i1 = jnp.take_along_axis(i_half[:, :, 0], pos // PEER_TOPK, axis=-1)
    i2 = jnp.take_along_axis(i_half[:, :, 1], pos % PEER_TOPK, axis=-1)
    experts = (i1 * PEER_KEYS + i2).reshape(T, PEER_HEADS * PEER_TOPK)
    gates = jax.nn.softmax(best, axis=-1).reshape(T, PEER_HEADS * PEER_TOPK).astype(n.dtype)

    out = _peer_experts(t, experts, gates, expert_u, expert_v)
    return out.reshape(bsz, L, d)


PEER_PICKS = PEER_HEADS * PEER_TOPK
PACKED_ROWS = PEER_KEYS * PEER_KEYS // 2
LANES = 128
SUBLANES = D_MODEL // LANES
PEER_TOKEN_TILE = 64
PEER_VMEM_LIMIT = 44 * 1024 * 1024


def _pack_expert_table(tab):
    bits = lax.bitcast_convert_type(tab.astype(jnp.bfloat16), jnp.uint16).astype(jnp.uint32)
    packed = bits[:PACKED_ROWS] | (bits[PACKED_ROWS:] << 16)
    return packed.reshape(PACKED_ROWS, SUBLANES, LANES)


def _gelu_exact(s):
    return 0.5 * s * (1.0 + lax.erf(s * (2.0 ** -0.5)))


def _expert_row(tab_ref, row, shift):
    w = tab_ref[row]
    return lax.bitcast_convert_type((w << shift.astype(jnp.uint32)) & jnp.uint32(0xFFFF0000),
                                    jnp.float32)


def _peer_hidden_body(row_ref, shift_ref, x_ref, gate_ref, utab_ref, act_ref, part_ref, s_ref):
    def token(t, carry):
        x = x_ref[t]
        for k in range(PEER_PICKS):
            r = _expert_row(utab_ref, row_ref[t, k], shift_ref[t, k])
            part_ref[pl.ds(k, 1), :] = jnp.sum(r * x, axis=0, keepdims=True)
        s_ref[pl.ds(t, 1), :] = jnp.sum(part_ref[...].T, axis=0, keepdims=True)
        return carry

    lax.fori_loop(0, PEER_TOKEN_TILE, token, 0)
    act_ref[...] = _gelu_exact(s_ref[...]) * gate_ref[...]


def _peer_out_body(row_ref, shift_ref, act_ref, vtab_ref, y_ref, abc_ref):
    def token(t, carry):
        a_row = act_ref[pl.ds(t, 1), :]
        abc_ref[...] = jnp.broadcast_to(a_row, (PEER_PICKS, LANES)).T
        acc = jnp.zeros((SUBLANES, LANES), jnp.float32)
        for k in range(PEER_PICKS):
            r = _expert_row(vtab_ref, row_ref[t, k], shift_ref[t, k])
            acc = acc + jnp.broadcast_to(abc_ref[pl.ds(k, 1), :], (SUBLANES, LANES)) * r
        y_ref[t] = acc
        return carry

    lax.fori_loop(0, PEER_TOKEN_TILE, token, 0)


def _peer_experts(t, experts, gates, expert_u, expert_v):
    T, d = t.shape
    tt = PEER_TOKEN_TILE
    assert T % tt == 0 and d == SUBLANES * LANES and experts.shape == (T, PEER_PICKS)
    rows = experts & (PACKED_ROWS - 1)
    shifts = (1 - experts // PACKED_ROWS) * 16
    x3 = t.reshape(T, SUBLANES, LANES)
    smem_spec = pl.BlockSpec((tt, PEER_PICKS), lambda i: (i, 0), memory_space=pltpu.SMEM)
    pick_spec = pl.BlockSpec((tt, PEER_PICKS), lambda i: (i, 0))
    tok_spec = pl.BlockSpec((tt, SUBLANES, LANES), lambda i: (i, 0, 0))
    table_spec = pl.BlockSpec(memory_space=pltpu.VMEM)
    params = pltpu.CompilerParams(dimension_semantics=("arbitrary",),
                                  vmem_limit_bytes=PEER_VMEM_LIMIT)
    act = pl.pallas_call(
        _peer_hidden_body,
        grid=(T // tt,),
        in_specs=[smem_spec, smem_spec, tok_spec, pick_spec, table_spec],
        out_specs=pick_spec,
        out_shape=jax.ShapeDtypeStruct((T, PEER_PICKS), jnp.float32),
        scratch_shapes=[pltpu.VMEM((PEER_PICKS, LANES), jnp.float32),
                        pltpu.VMEM((tt, PEER_PICKS), jnp.float32)],
        compiler_params=params,
        name="peer_hidden",
    )(rows, shifts, x3, gates, _pack_expert_table(expert_u))
    y = pl.pallas_call(
        _peer_out_body,
        grid=(T // tt,),
        in_specs=[smem_spec, smem_spec, pick_spec, table_spec],
        out_specs=tok_spec,
        out_shape=jax.ShapeDtypeStruct((T, SUBLANES, LANES), jnp.float32),
        scratch_shapes=[pltpu.VMEM((PEER_PICKS, LANES), jnp.float32)],
        compiler_params=params,
        name="peer_out",
    )(rows, shifts, act, _pack_expert_table(expert_v))
    return y.reshape(T, d)


def _final_norm_body(h_ref, g_ref, o_ref):
    xf = h_ref[...]
    ms = jnp.mean(xf * xf, axis=-1, keepdims=True)
    o_ref[...] = xf * lax.rsqrt(ms + EPS) * g_ref[...]


def _final_norm(h, g):
    rows, d = h.shape
    tile = 512
    return pl.pallas_call(
        _final_norm_body,
        grid=(rows // tile,),
        in_specs=[pl.BlockSpec((tile, d), lambda i: (i, 0)),
                  pl.BlockSpec((1, d), lambda i: (0, 0))],
        out_specs=pl.BlockSpec((tile, d), lambda i: (i, 0)),
        out_shape=jax.ShapeDtypeStruct((rows, d), jnp.float32),
        name="final_norm",
    )(h, g.reshape(1, d))


def kernel(x, meta_tokens, g_mix, w_in, conv_ssd_w, conv_ssd_b, dt_bias, a_log, d_skip,
           ssd_norm_w, conv_sc_w, w_out, g_ffn, w_q, sub_keys, expert_u, expert_v, g_final):
    bsz, seq, d = x.shape
    meta = jnp.broadcast_to(meta_tokens[None].astype(x.dtype), (bsz, N_META, d))
    h = jnp.concatenate([meta, x], axis=1)
    for l in range(g_mix.shape[0]):
        h = h + _hybrid_mixer(_rmsnorm(h, g_mix[l]), w_in[l], conv_ssd_w[l], conv_ssd_b[l],
                              dt_bias[l], a_log[l], d_skip[l], ssd_norm_w[l], conv_sc_w[l],
                              w_out[l])
        h = h + _peer_ffn(_rmsnorm(h, g_ffn[l]), w_q[l], sub_keys[l], expert_u[l], expert_v[l])
    out = _final_norm(h[:, N_META:].reshape(bsz * seq, d), g_final)
    return out.reshape(bsz, seq, d)
```

```python
import numpy as np

import jax
import jax.numpy as jnp
from jax import lax
from jax.experimental import pallas as pl
from jax.experimental.pallas import tpu as pltpu

D_MODEL = 1024
N_META = 16
D_MIX = 2 * D_MODEL
SSD_HEAD_DIM = 64
SSD_HEADS = 24
D_SSD = SSD_HEADS * SSD_HEAD_DIM
SSD_GROUPS = 4
SSD_STATE = 128
SSD_CONV = 4
SSD_CHUNK = 128
D_XBC = D_SSD + 2 * SSD_GROUPS * SSD_STATE
D_SC = D_MIX - D_SSD
SC_CONV = 3
D_IN = D_SSD + D_XBC + SSD_HEADS + 3 * D_SC
PEER_HEADS = 8
PEER_KEYS = 128
PEER_TOPK = 16
PEER_QDIM = 256
PEER_HALF = PEER_QDIM // 2
EPS = 1e-6

LANES = 128
VREG_SUBLANES = 8


def _rmsnorm(x, g):
    xf = x.astype(jnp.float32)
    xf = xf * lax.rsqrt(jnp.mean(xf * xf, axis=-1, keepdims=True) + EPS)
    return (xf * g.astype(jnp.float32)).astype(x.dtype)


def _causal_dwconv(x, w):
    width, ch = w.shape
    return lax.conv_general_dilated(
        x, w[:, None, :].astype(x.dtype), window_strides=(1,), padding=[(width - 1, 0)],
        dimension_numbers=('NWC', 'WIO', 'NWC'), feature_group_count=ch)


def _segsum(a):
    t = a.shape[-1]
    ar = jnp.broadcast_to(a[..., None], a.shape + (t,))
    ar = jnp.where(jnp.tril(jnp.ones((t, t), bool), -1), ar, 0.0)
    cs = jnp.cumsum(ar, axis=-2)
    return jnp.where(jnp.tril(jnp.ones((t, t), bool), 0), cs, -jnp.inf)


def _ssd_scan(x, dt, A, Bm, Cm):
    b, l, h, p = x.shape
    g, n = Bm.shape[2], Bm.shape[3]
    r, q = h // g, SSD_CHUNK
    c = l // q
    xc = x.reshape(b, c, q, g, r, p)
    Bc = Bm.reshape(b, c, q, g, n)
    Cc = Cm.reshape(b, c, q, g, n)
    dtc = dt.reshape(b, c, q, g, r)
    a = (dtc * A.reshape(g, r)).transpose(0, 3, 4, 1, 2)
    x_dt = xc * dtc[..., None]
    a_cs = jnp.cumsum(a, axis=-1)
    decay_in = jnp.exp(_segsum(a))
    cb = jnp.einsum('bclgn,bcsgn->bgcls', Cc, Bc)
    y_diag = jnp.einsum('bgcls,bgrcls,bcsgrp->bclgrp', cb, decay_in, x_dt)
    decay_states = jnp.exp(a_cs[..., -1:] - a_cs)
    states = jnp.einsum('bcsgn,bgrcs,bcsgrp->bcgrpn', Bc, decay_states, x_dt)
    states = jnp.concatenate([jnp.zeros_like(states[:, :1]), states], axis=1)
    chunk_a = jnp.pad(a_cs[..., -1], ((0, 0), (0, 0), (0, 0), (1, 0)))
    decay_chunk = jnp.exp(_segsum(chunk_a))
    new_states = jnp.einsum('bgrzc,bcgrpn->bzgrpn', decay_chunk, states)
    prev_states = new_states[:, :-1]
    y_off = jnp.einsum('bclgn,bcgrpn,bgrcl->bclgrp', Cc, prev_states, jnp.exp(a_cs))
    return (y_diag + y_off).reshape(b, l, h, p)


def _hybrid_mixer(n, w_in, conv_ssd_w, conv_ssd_b, dt_bias, a_log, d_skip, ssd_norm_w,
                  conv_sc_w, w_out):
    bsz, L, _ = n.shape
    proj = n @ w_in.astype(n.dtype)
    s1 = D_SSD
    s2 = s1 + D_XBC
    s3 = s2 + SSD_HEADS
    s4 = s3 + D_SC
    s5 = s4 + D_SC
    z, xbc, dt_raw, sc_b, sc_c, sc_x = jnp.split(proj, [s1, s2, s3, s4, s5], axis=-1)
    xbc = jax.nn.silu(_causal_dwconv(xbc, conv_ssd_w) + conv_ssd_b.astype(n.dtype))
    xs, Bm, Cm = jnp.split(xbc, [D_SSD, D_SSD + SSD_GROUPS * SSD_STATE], axis=-1)
    xs_h = xs.reshape(bsz, L, SSD_HEADS, SSD_HEAD_DIM)
    dt = jax.nn.softplus(dt_raw.astype(jnp.float32) + dt_bias.astype(jnp.float32))
    A = -jnp.exp(a_log.astype(jnp.float32))
    pad_len = SSD_CHUNK - N_META

    def lpad(t):
        return jnp.pad(t, ((0, 0), (pad_len, 0)) + ((0, 0),) * (t.ndim - 2))

    y = _ssd_scan(lpad(xs_h.astype(jnp.float32)), lpad(dt), A,
                  lpad(Bm.reshape(bsz, L, SSD_GROUPS, SSD_STATE).astype(jnp.float32)),
                  lpad(Cm.reshape(bsz, L, SSD_GROUPS, SSD_STATE).astype(jnp.float32)))
    y = y[:, pad_len:] + d_skip.astype(jnp.float32)[:, None] * xs_h.astype(jnp.float32)
    y = y.reshape(bsz, L, D_SSD) * jax.nn.silu(z.astype(jnp.float32))
    y_ssd = _rmsnorm(y.reshape(bsz, L, SSD_GROUPS, D_SSD // SSD_GROUPS),
                     ssd_norm_w.reshape(SSD_GROUPS, D_SSD // SSD_GROUPS))
    y_ssd = y_ssd.reshape(bsz, L, D_SSD).astype(n.dtype)
    y_sc = sc_b * _causal_dwconv(sc_c * sc_x, conv_sc_w)
    return jnp.concatenate([y_ssd, y_sc], axis=-1) @ w_out.astype(n.dtype)


PEER_PICKS = PEER_HEADS * PEER_TOPK
PACKED_ROWS = PEER_KEYS * PEER_KEYS // 2
ROUTE_TOKEN_TILE = 256

_STAIR = [(a, b) for a in range(PEER_TOPK) for b in range(PEER_TOPK)
          if (a + 1) * (b + 1) <= PEER_TOPK]
STAIR_ROWS = -(-len(_STAIR) // VREG_SUBLANES) * VREG_SUBLANES


def _stair_constants():
    sel_a = np.zeros((STAIR_ROWS, PEER_TOPK), np.float32)
    sel_b = np.zeros((STAIR_ROWS, PEER_TOPK), np.float32)
    pad = np.full((STAIR_ROWS, LANES), -np.inf, np.float32)
    for c, (a, b) in enumerate(_STAIR):
        sel_a[c, a] = 1.0
        sel_b[c, b] = 1.0
        pad[c, :] = 0.0
    return jnp.asarray(sel_a), jnp.asarray(sel_b), jnp.asarray(pad)


def _top16_rows(c, val_ref, idx_ref):
    rows = c.shape[0]
    iota = lax.broadcasted_iota(jnp.int32, c.shape, 0)
    for r in range(PEER_TOPK):
        m = jnp.max(c, axis=0, keepdims=True)
        idx = jnp.min(jnp.where(c == m, iota, rows), axis=0, keepdims=True)
        val_ref[pl.ds(r, 1), :] = m
        idx_ref[pl.ds(r, 1), :] = idx
        c = jnp.where(iota == idx, -jnp.inf, c)


def _peer_route_body(h_ref, g_ref, wq_ref, keys_ref, sela_ref, selb_ref, pad_ref,
                     xn_ref, rows_ref, shift_ref, gate_ref,
                     q_ref, v_ref, i_ref, best_ref, pos_ref, e_ref, p_ref):
    hx = h_ref[...]
    xn = hx * lax.rsqrt(jnp.mean(hx * hx, axis=-1, keepdims=True) + EPS) * g_ref[...]
    xn_ref[...] = xn
    xb = xn.astype(jnp.bfloat16)
    for hh in range(PEER_HEADS):
        qh = jnp.dot(xb, wq_ref[:, hh * PEER_QDIM:(hh + 1) * PEER_QDIM],
                     preferred_element_type=jnp.float32).astype(jnp.bfloat16)
        q_ref[2 * hh] = qh[:, :PEER_HALF]
        q_ref[2 * hh + 1] = qh[:, PEER_HALF:]

    for lt in range(ROUTE_TOKEN_TILE // LANES):
        tok = pl.ds(lt * LANES, LANES)

        def head(hh, carry):
            for half in range(2):
                j = 2 * hh + half
                s_t = lax.dot_general(keys_ref[j], q_ref[j, tok, :], (((1,), (1,)), ((), ())),
                                      preferred_element_type=jnp.float32)
                _top16_rows(s_t, v_ref.at[half], i_ref.at[half])
            v1, v2 = v_ref[0], v_ref[1]
            i1 = i_ref[0].astype(jnp.float32)
            i2 = i_ref[1].astype(jnp.float32)
            hi = lax.Precision.HIGHEST
            cand = (jnp.dot(sela_ref[...], v1, precision=hi, preferred_element_type=jnp.float32)
                    + jnp.dot(selb_ref[...], v2, precision=hi, preferred_element_type=jnp.float32)
                    + pad_ref[...])
            ea = jnp.dot(sela_ref[...], i1, preferred_element_type=jnp.float32)
            eb = jnp.dot(selb_ref[...], i2, preferred_element_type=jnp.float32)
            cand_e = (ea * PEER_KEYS + eb).astype(jnp.int32)
            _top16_rows(cand, best_ref, pos_ref)
            best = best_ref[...]
            pos = pos_ref[...]
            crow = lax.broadcasted_iota(jnp.int32, cand.shape, 0)
            picks = []
            for r in range(PEER_TOPK):
                hit = crow == pos[r:r + 1, :]
                picks.append(jnp.sum(jnp.where(hit, cand_e, 0), axis=0, keepdims=True))
            w = jnp.exp(best - best[0:1, :])
            gates = w / jnp.sum(w, axis=0, keepdims=True)
            base = pl.multiple_of(hh * PEER_TOPK, PEER_TOPK)
            p_ref[pl.ds(base, PEER_TOPK), :] = gates
            e_ref[pl.ds(base, PEER_TOPK), :] = jnp.concatenate(picks, axis=0)
            return carry

        lax.fori_loop(0, PEER_HEADS, head, 0)
        e_t = e_ref[...].T
        rows_ref[tok, :] = (e_t & (PACKED_ROWS - 1)) * VREG_SUBLANES
        shift_ref[tok, :] = (1 - (e_t >> 13)) * 16
        gate_ref[tok, :] = p_ref[...].T


def _peer_route(h, g_ffn, w_q, sub_keys):
    T, d = h.shape
    tt = ROUTE_TOKEN_TILE
    assert T % tt == 0 and PACKED_ROWS == 1 << 13
    sel_a, sel_b, pad = _stair_constants()
    keys = sub_keys.reshape(PEER_HEADS * 2, PEER_KEYS, PEER_HALF).astype(jnp.bfloat16)
    tok_spec = pl.BlockSpec((tt, d), lambda i: (i, 0))
    pick_spec = pl.BlockSpec((tt, PEER_PICKS), lambda i: (i, 0))

    def whole(shape):
        return pl.BlockSpec(shape, lambda i: (0,) * len(shape))

    return pl.pallas_call(
        _peer_route_body,
        grid=(T // tt,),
        in_specs=[tok_spec, whole((1, d)), whole((d, PEER_HEADS * PEER_QDIM)),
                  whole(keys.shape), whole(sel_a.shape), whole(sel_b.shape), whole(pad.shape)],
        out_specs=[tok_spec, pick_spec, pick_spec, pick_spec],
        out_shape=[jax.ShapeDtypeStruct((T, d), jnp.float32),
                   jax.ShapeDtypeStruct((T, PEER_PICKS), jnp.int32),
                   jax.ShapeDtypeStruct((T, PEER_PICKS), jnp.int32),
                   jax.ShapeDtypeStruct((T, PEER_PICKS), jnp.float32)],
        scratch_shapes=[pltpu.VMEM((2 * PEER_HEADS, tt, PEER_HALF), jnp.bfloat16),
                        pltpu.VMEM((2, PEER_TOPK, LANES), jnp.float32),
                        pltpu.VMEM((2, PEER_TOPK, LANES), jnp.int32),
                        pltpu.VMEM((PEER_TOPK, LANES), jnp.float32),
                        pltpu.VMEM((PEER_TOPK, LANES), jnp.int32),
                        pltpu.VMEM((PEER_PICKS, LANES), jnp.int32),
                        pltpu.VMEM((PEER_PICKS, LANES), jnp.float32)],
        compiler_params=pltpu.CompilerParams(dimension_semantics=("arbitrary",)),
        name="peer_route",
    )(h, g_ffn.reshape(1, d), w_q.astype(jnp.bfloat16), keys, sel_a, sel_b, pad)


ROW_SUBLANES = D_MODEL // LANES
PEER_TOKEN_TILE = 64
PEER_VMEM_LIMIT = 44 * 1024 * 1024
HIGH_HALF_MASK = 0xFFFF0000
_BITREV8 = (0, 4, 2, 6, 1, 5, 3, 7)


def _pack_expert_table(tab):
    bits = lax.bitcast_convert_type(tab.astype(jnp.bfloat16), jnp.uint16).astype(jnp.uint32)
    packed = bits[:PACKED_ROWS] | (bits[PACKED_ROWS:] << 16)
    return packed.reshape(PACKED_ROWS * ROW_SUBLANES, LANES)


def _gelu_exact(s):
    return 0.5 * s * (1.0 + lax.erf(s * (2.0 ** -0.5)))


def _lane_replicated(row, dst_ref):
    dst_ref[...] = jnp.broadcast_to(row, (PEER_PICKS, LANES)).T


def _expert_row(tab_ref, row8, shift_ref, k):
    w = tab_ref[pl.ds(pl.multiple_of(row8, ROW_SUBLANES), ROW_SUBLANES), :]
    s = jnp.broadcast_to(shift_ref[pl.ds(k, 1), :], (ROW_SUBLANES, LANES)).astype(jnp.uint32)
    return lax.bitcast_convert_type((w << s) & jnp.uint32(HIGH_HALF_MASK), jnp.float32)


def _fold_pair(a, b, shift, keep_a):
    if 2 * shift == ROW_SUBLANES:
        return jnp.where(keep_a, a, b) + pltpu.roll(jnp.where(keep_a, b, a), shift, axis=0)
    return (jnp.where(keep_a, a, pltpu.roll(b, shift, axis=0))
            + jnp.where(keep_a, pltpu.roll(a, ROW_SUBLANES - shift, axis=0), b))


def _sublane_sums(tiles, masks):
    level = [tiles[_BITREV8[s]] for s in range(ROW_SUBLANES)]
    for shift, keep in masks:
        level = [_fold_pair(level[2 * i], level[2 * i + 1], shift, keep)
                 for i in range(len(level) // 2)]
    return level[0]


def _peer_hidden_body(row_ref, shift_ref, x_ref, gate_ref, utab_ref, act_ref,
                      part_ref, sh_ref, s_ref):
    sub = lax.broadcasted_iota(jnp.int32, (ROW_SUBLANES, LANES), 0)
    masks = [(4, (sub & 4) == 0), (2, (sub & 2) == 0), (1, (sub & 1) == 0)]

    def token(t, carry):
        x = x_ref[t]
        _lane_replicated(shift_ref[pl.ds(t, 1), :], sh_ref)
        for g in range(PEER_PICKS // ROW_SUBLANES):
            prods = [_expert_row(utab_ref, row_ref[t, g * ROW_SUBLANES + j], sh_ref,
                                 g * ROW_SUBLANES + j) * x
                     for j in range(ROW_SUBLANES)]
            part_ref[pl.ds(g * ROW_SUBLANES, ROW_SUBLANES), :] = _sublane_sums(prods, masks)
        s_ref[pl.ds(t, 1), :] = jnp.sum(part_ref[...].T, axis=0, keepdims=True)
        return carry

    lax.fori_loop(0, PEER_TOKEN_TILE, token, 0)
    act_ref[...] = _gelu_exact(s_ref[...]) * gate_ref[...]


def _peer_out_body(row_ref, shift_ref, act_ref, vtab_ref, y_ref, abc_ref, sh_ref):
    def token(t, carry):
        _lane_replicated(act_ref[pl.ds(t, 1), :], abc_ref)
        _lane_replicated(shift_ref[pl.ds(t, 1), :], sh_ref)
        acc = jnp.zeros((ROW_SUBLANES, LANES), jnp.float32)
        for k in range(PEER_PICKS):
            r = _expert_row(vtab_ref, row_ref[t, k], sh_ref, k)
            acc = acc + jnp.broadcast_to(abc_ref[pl.ds(k, 1), :], (ROW_SUBLANES, LANES)) * r
        y_ref[t] = acc
        return carry

    lax.fori_loop(0, PEER_TOKEN_TILE, token, 0)


def _peer_experts(xn, rows8, shifts, gates, expert_u, expert_v):
    T, d = xn.shape
    tt = PEER_TOKEN_TILE
    assert T % tt == 0 and d == ROW_SUBLANES * LANES and ROW_SUBLANES == VREG_SUBLANES
    x3 = xn.reshape(T, ROW_SUBLANES, LANES)
    smem_spec = pl.BlockSpec((tt, PEER_PICKS), lambda i: (i, 0), memory_space=pltpu.SMEM)
    pick_spec = pl.BlockSpec((tt, PEER_PICKS), lambda i: (i, 0))
    tok_spec = pl.BlockSpec((tt, ROW_SUBLANES, LANES), lambda i: (i, 0, 0))
    table_spec = pl.BlockSpec(memory_space=pltpu.VMEM)
    params = pltpu.CompilerParams(dimension_semantics=("arbitrary",),
                                  vmem_limit_bytes=PEER_VMEM_LIMIT)
    act = pl.pallas_call(
        _peer_hidden_body,
        grid=(T // tt,),
        in_specs=[smem_spec, pick_spec, tok_spec, pick_spec, table_spec],
        out_specs=pick_spec,
        out_shape=jax.ShapeDtypeStruct((T, PEER_PICKS), jnp.float32),
        scratch_shapes=[pltpu.VMEM((PEER_PICKS, LANES), jnp.float32),
                        pltpu.VMEM((PEER_PICKS, LANES), jnp.int32),
                        pltpu.VMEM((tt, PEER_PICKS), jnp.float32)],
        compiler_params=params,
        name="peer_hidden",
    )(rows8, shifts, x3, gates, _pack_expert_table(expert_u))
    y = pl.pallas_call(
        _peer_out_body,
        grid=(T // tt,),
        in_specs=[smem_spec, pick_spec, pick_spec, table_spec],
        out_specs=tok_spec,
        out_shape=jax.ShapeDtypeStruct((T, ROW_SUBLANES, LANES), jnp.float32),
        scratch_shapes=[pltpu.VMEM((PEER_PICKS, LANES), jnp.float32),
                        pltpu.VMEM((PEER_PICKS, LANES), jnp.int32)],
        compiler_params=params,
        name="peer_out",
    )(rows8, shifts, act, _pack_expert_table(expert_v))
    return y.reshape(T, d)


def _peer_ffn(h, g_ffn, w_q, sub_keys, expert_u, expert_v):
    xn, rows8, shifts, gates = _peer_route(h, g_ffn, w_q, sub_keys)
    return _peer_experts(xn, rows8, shifts, gates, expert_u, expert_v)


def _final_norm_body(h_ref, g_ref, o_ref):
    xf = h_ref[...]
    ms = jnp.mean(xf * xf, axis=-1, keepdims=True)
    o_ref[...] = xf * lax.rsqrt(ms + EPS) * g_ref[...]


def _final_norm(h, g):
    rows, d = h.shape
    tile = 512
    return pl.pallas_call(
        _final_norm_body,
        grid=(rows // tile,),
        in_specs=[pl.BlockSpec((tile, d), lambda i: (i, 0)),
                  pl.BlockSpec((1, d), lambda i: (0, 0))],
        out_specs=pl.BlockSpec((tile, d), lambda i: (i, 0)),
        out_shape=jax.ShapeDtypeStruct((rows, d), jnp.float32),
        name="final_norm",
    )(h, g.reshape(1, d))


def kernel(x, meta_tokens, g_mix, w_in, conv_ssd_w, conv_ssd_b, dt_bias, a_log, d_skip,
           ssd_norm_w, conv_sc_w, w_out, g_ffn, w_q, sub_keys, expert_u, expert_v, g_final):
    bsz, seq, d = x.shape
    meta = jnp.broadcast_to(meta_tokens[None].astype(x.dtype), (bsz, N_META, d))
    h = jnp.concatenate([meta, x], axis=1)
    L = h.shape[1]
    for l in range(g_mix.shape[0]):
        h = h + _hybrid_mixer(_rmsnorm(h, g_mix[l]), w_in[l], conv_ssd_w[l], conv_ssd_b[l],
                              dt_bias[l], a_log[l], d_skip[l], ssd_norm_w[l], conv_sc_w[l],
                              w_out[l])
        ffn = _peer_ffn(h.reshape(bsz * L, d), g_ffn[l], w_q[l], sub_keys[l],
                        expert_u[l], expert_v[l])
        h = h + ffn.reshape(bsz, L, d)
    out = _final_norm(h[:, N_META:].reshape(bsz * seq, d), g_final)
    return out.reshape(bsz, seq, d)
```

```python
import numpy as np

import jax
import jax.numpy as jnp
from jax import lax
from jax.experimental import pallas as pl
from jax.experimental.pallas import tpu as pltpu

D_MODEL = 1024
N_META = 16
D_MIX = 2 * D_MODEL
SSD_HEAD_DIM = 64
SSD_HEADS = 24
D_SSD = SSD_HEADS * SSD_HEAD_DIM
SSD_GROUPS = 4
SSD_STATE = 128
SSD_CONV = 4
SSD_CHUNK = 128
D_XBC = D_SSD + 2 * SSD_GROUPS * SSD_STATE
D_SC = D_MIX - D_SSD
SC_CONV = 3
D_IN = D_SSD + D_XBC + SSD_HEADS + 3 * D_SC
PEER_HEADS = 8
PEER_KEYS = 128
PEER_TOPK = 16
PEER_QDIM = 256
PEER_HALF = PEER_QDIM // 2
EPS = 1e-6

LANES = 128
VREG_SUBLANES = 8


def _rmsnorm(x, g):
    xf = x.astype(jnp.float32)
    xf = xf * lax.rsqrt(jnp.mean(xf * xf, axis=-1, keepdims=True) + EPS)
    return (xf * g.astype(jnp.float32)).astype(x.dtype)


def _causal_dwconv(x, w):
    width, ch = w.shape
    return lax.conv_general_dilated(
        x, w[:, None, :].astype(x.dtype), window_strides=(1,), padding=[(width - 1, 0)],
        dimension_numbers=('NWC', 'WIO', 'NWC'), feature_group_count=ch)


def _segsum(a):
    t = a.shape[-1]
    ar = jnp.broadcast_to(a[..., None], a.shape + (t,))
    ar = jnp.where(jnp.tril(jnp.ones((t, t), bool), -1), ar, 0.0)
    cs = jnp.cumsum(ar, axis=-2)
    return jnp.where(jnp.tril(jnp.ones((t, t), bool), 0), cs, -jnp.inf)


def _ssd_scan(x, dt, A, Bm, Cm):
    b, l, h, p = x.shape
    g, n = Bm.shape[2], Bm.shape[3]
    r, q = h // g, SSD_CHUNK
    c = l // q
    xc = x.reshape(b, c, q, g, r, p)
    Bc = Bm.reshape(b, c, q, g, n)
    Cc = Cm.reshape(b, c, q, g, n)
    dtc = dt.reshape(b, c, q, g, r)
    a = (dtc * A.reshape(g, r)).transpose(0, 3, 4, 1, 2)
    x_dt = xc * dtc[..., None]
    a_cs = jnp.cumsum(a, axis=-1)
    decay_in = jnp.exp(_segsum(a))
    cb = jnp.einsum('bclgn,bcsgn->bgcls', Cc, Bc)
    y_diag = jnp.einsum('bgcls,bgrcls,bcsgrp->bclgrp', cb, decay_in, x_dt)
    decay_states = jnp.exp(a_cs[..., -1:] - a_cs)
    states = jnp.einsum('bcsgn,bgrcs,bcsgrp->bcgrpn', Bc, decay_states, x_dt)
    states = jnp.concatenate([jnp.zeros_like(states[:, :1]), states], axis=1)
    chunk_a = jnp.pad(a_cs[..., -1], ((0, 0), (0, 0), (0, 0), (1, 0)))
    decay_chunk = jnp.exp(_segsum(chunk_a))
    new_states = jnp.einsum('bgrzc,bcgrpn->bzgrpn', decay_chunk, states)
    prev_states = new_states[:, :-1]
    y_off = jnp.einsum('bclgn,bcgrpn,bgrcl->bclgrp', Cc, prev_states, jnp.exp(a_cs))
    return (y_diag + y_off).reshape(b, l, h, p)


def _hybrid_mixer(n, w_in, conv_ssd_w, conv_ssd_b, dt_bias, a_log, d_skip, ssd_norm_w,
                  conv_sc_w, w_out):
    bsz, L, _ = n.shape
    proj = n @ w_in.astype(n.dtype)
    s1 = D_SSD
    s2 = s1 + D_XBC
    s3 = s2 + SSD_HEADS
    s4 = s3 + D_SC
    s5 = s4 + D_SC
    z, xbc, dt_raw, sc_b, sc_c, sc_x = jnp.split(proj, [s1, s2, s3, s4, s5], axis=-1)
    xbc = jax.nn.silu(_causal_dwconv(xbc, conv_ssd_w) + conv_ssd_b.astype(n.dtype))
    xs, Bm, Cm = jnp.split(xbc, [D_SSD, D_SSD + SSD_GROUPS * SSD_STATE], axis=-1)
    xs_h = xs.reshape(bsz, L, SSD_HEADS, SSD_HEAD_DIM)
    dt = jax.nn.softplus(dt_raw.astype(jnp.float32) + dt_bias.astype(jnp.float32))
    A = -jnp.exp(a_log.astype(jnp.float32))
    pad_len = SSD_CHUNK - N_META

    def lpad(t):
        return jnp.pad(t, ((0, 0), (pad_len, 0)) + ((0, 0),) * (t.ndim - 2))

    y = _ssd_scan(lpad(xs_h.astype(jnp.float32)), lpad(dt), A,
                  lpad(Bm.reshape(bsz, L, SSD_GROUPS, SSD_STATE).astype(jnp.float32)),
                  lpad(Cm.reshape(bsz, L, SSD_GROUPS, SSD_STATE).astype(jnp.float32)))
    y = y[:, pad_len:] + d_skip.astype(jnp.float32)[:, None] * xs_h.astype(jnp.float32)
    y = y.reshape(bsz, L, D_SSD) * jax.nn.silu(z.astype(jnp.float32))
    y_ssd = _rmsnorm(y.reshape(bsz, L, SSD_GROUPS, D_SSD // SSD_GROUPS),
                     ssd_norm_w.reshape(SSD_GROUPS, D_SSD // SSD_GROUPS))
    y_ssd = y_ssd.reshape(bsz, L, D_SSD).astype(n.dtype)
    y_sc = sc_b * _causal_dwconv(sc_c * sc_x, conv_sc_w)
    return jnp.concatenate([y_ssd, y_sc], axis=-1) @ w_out.astype(n.dtype)


PEER_PICKS = PEER_HEADS * PEER_TOPK
PACKED_ROWS = PEER_KEYS * PEER_KEYS // 2
ROUTE_TOKEN_TILE = 256

_STAIR = [(a, b) for a in range(PEER_TOPK) for b in range(PEER_TOPK)
          if (a + 1) * (b + 1) <= PEER_TOPK]
STAIR_ROWS = -(-len(_STAIR) // VREG_SUBLANES) * VREG_SUBLANES


def _stair_constants():
    sel_a = np.zeros((STAIR_ROWS, PEER_TOPK), np.float32)
    sel_b = np.zeros((STAIR_ROWS, PEER_TOPK), np.float32)
    pad = np.full((STAIR_ROWS, LANES), -np.inf, np.float32)
    for c, (a, b) in enumerate(_STAIR):
        sel_a[c, a] = 1.0
        sel_b[c, b] = 1.0
        pad[c, :] = 0.0
    return jnp.asarray(sel_a), jnp.asarray(sel_b), jnp.asarray(pad)


def _top16_rows(c, val_ref, idx_ref):
    rows = c.shape[0]
    iota = lax.broadcasted_iota(jnp.int32, c.shape, 0).astype(jnp.float32)
    for r in range(PEER_TOPK):
        m = jnp.max(c, axis=0, keepdims=True)
        idx = jnp.min(jnp.where(c == m, iota, float(rows)), axis=0, keepdims=True)
        val_ref[pl.ds(r, 1), :] = m
        idx_ref[pl.ds(r, 1), :] = idx
        c = jnp.where(iota == idx, -jnp.inf, c)


def _peer_route_body(h_ref, g_ref, wq_ref, keys_ref, sela_ref, selb_ref, pad_ref,
                     xn_ref, rows_ref, shift_ref, gate_ref,
                     q_ref, v_ref, i_ref, best_ref, pos_ref, e_ref, p_ref):
    hx = h_ref[...]
    xn = hx * lax.rsqrt(jnp.mean(hx * hx, axis=-1, keepdims=True) + EPS) * g_ref[...]
    xn_ref[...] = xn
    xb = xn.astype(jnp.bfloat16)
    for hh in range(PEER_HEADS):
        qh = jnp.dot(xb, wq_ref[:, hh * PEER_QDIM:(hh + 1) * PEER_QDIM],
                     preferred_element_type=jnp.float32).astype(jnp.bfloat16)
        q_ref[2 * hh] = qh[:, :PEER_HALF]
        q_ref[2 * hh + 1] = qh[:, PEER_HALF:]

    for lt in range(ROUTE_TOKEN_TILE // LANES):
        tok = pl.ds(lt * LANES, LANES)

        def head(hh, carry):
            for half in range(2):
                j = 2 * hh + half
                s_t = lax.dot_general(keys_ref[j], q_ref[j, tok, :], (((1,), (1,)), ((), ())),
                                      preferred_element_type=jnp.float32)
                _top16_rows(s_t, v_ref.at[half], i_ref.at[half])
            v1, v2 = v_ref[0], v_ref[1]
            i1, i2 = i_ref[0], i_ref[1]
            hi = lax.Precision.HIGHEST
            cand = (jnp.dot(sela_ref[...], v1, precision=hi, preferred_element_type=jnp.float32)
                    + jnp.dot(selb_ref[...], v2, precision=hi, preferred_element_type=jnp.float32)
                    + pad_ref[...])
            ea = jnp.dot(sela_ref[...], i1, preferred_element_type=jnp.float32)
            eb = jnp.dot(selb_ref[...], i2, preferred_element_type=jnp.float32)
            cand_e = (ea * PEER_KEYS + eb).astype(jnp.int32)
            _top16_rows(cand, best_ref, pos_ref)
            best = best_ref[...]
            pos = pos_ref[...]
            crow = lax.broadcasted_iota(jnp.int32, cand.shape, 0).astype(jnp.float32)
            picks = []
            for r in range(PEER_TOPK):
                hit = crow == pos[r:r + 1, :]
                picks.append(jnp.sum(jnp.where(hit, cand_e, 0), axis=0, keepdims=True))
            w = jnp.exp(best - best[0:1, :])
            gates = w / jnp.sum(w, axis=0, keepdims=True)
            base = pl.multiple_of(hh * PEER_TOPK, PEER_TOPK)
            p_ref[pl.ds(base, PEER_TOPK), :] = gates
            e_ref[pl.ds(base, PEER_TOPK), :] = jnp.concatenate(picks, axis=0)
            return carry

        lax.fori_loop(0, PEER_HEADS, head, 0)
        e_t = e_ref[...].T
        rows_ref[tok, :] = (e_t & (PACKED_ROWS - 1)) * VREG_SUBLANES
        shift_ref[tok, :] = (1 - (e_t >> 13)) * 16
        gate_ref[tok, :] = p_ref[...].T


def _peer_route(h, g_ffn, w_q, sub_keys):
    T, d = h.shape
    tt = ROUTE_TOKEN_TILE
    assert T % tt == 0 and PACKED_ROWS == 1 << 13
    sel_a, sel_b, pad = _stair_constants()
    keys = sub_keys.reshape(PEER_HEADS * 2, PEER_KEYS, PEER_HALF).astype(jnp.bfloat16)
    tok_spec = pl.BlockSpec((tt, d), lambda i: (i, 0))
    pick_spec = pl.BlockSpec((tt, PEER_PICKS), lambda i: (i, 0))

    def whole(shape):
        return pl.BlockSpec(shape, lambda i: (0,) * len(shape))

    return pl.pallas_call(
        _peer_route_body,
        grid=(T // tt,),
        in_specs=[tok_spec, whole((1, d)), whole((d, PEER_HEADS * PEER_QDIM)),
                  whole(keys.shape), whole(sel_a.shape), whole(sel_b.shape), whole(pad.shape)],
        out_specs=[tok_spec, pick_spec, pick_spec, pick_spec],
        out_shape=[jax.ShapeDtypeStruct((T, d), jnp.float32),
                   jax.ShapeDtypeStruct((T, PEER_PICKS), jnp.int32),
                   jax.ShapeDtypeStruct((T, PEER_PICKS), jnp.int32),
                   jax.ShapeDtypeStruct((T, PEER_PICKS), jnp.float32)],
        scratch_shapes=[pltpu.VMEM((2 * PEER_HEADS, tt, PEER_HALF), jnp.bfloat16),
                        pltpu.VMEM((2, PEER_TOPK, LANES), jnp.float32),
                        pltpu.VMEM((2, PEER_TOPK, LANES), jnp.float32),
                        pltpu.VMEM((PEER_TOPK, LANES), jnp.float32),
                        pltpu.VMEM((PEER_TOPK, LANES), jnp.float32),
                        pltpu.VMEM((PEER_PICKS, LANES), jnp.int32),
                        pltpu.VMEM((PEER_PICKS, LANES), jnp.float32)],
        compiler_params=pltpu.CompilerParams(dimension_semantics=("arbitrary",)),
        name="peer_route",
    )(h, g_ffn.reshape(1, d), w_q.astype(jnp.bfloat16), keys, sel_a, sel_b, pad)


ROW_SUBLANES = D_MODEL // LANES
PEER_TOKEN_TILE = 64
PACKED_TABLE_BYTES = PACKED_ROWS * D_MODEL * 4
LANE_REPLICATED_BYTES = PEER_TOKEN_TILE * PEER_PICKS * LANES * 4
PEER_VMEM_LIMIT = PACKED_TABLE_BYTES + 2 * LANE_REPLICATED_BYTES + 8 * 1024 * 1024
HIGH_HALF_MASK = 0xFFFF0000
_BITREV8 = (0, 4, 2, 6, 1, 5, 3, 7)


def _pack_expert_table(tab):
    bits = lax.bitcast_convert_type(tab.astype(jnp.bfloat16), jnp.uint16).astype(jnp.uint32)
    packed = bits[:PACKED_ROWS] | (bits[PACKED_ROWS:] << 16)
    return packed.reshape(PACKED_ROWS * ROW_SUBLANES, LANES)


def _gelu_exact(s):
    return 0.5 * s * (1.0 + lax.erf(s * (2.0 ** -0.5)))


TRANSPOSE_BATCH = 8


def _lane_replicated(src_ref, dst_ref):
    def batch(i, carry):
        for u in range(TRANSPOSE_BATCH):
            t = i * TRANSPOSE_BATCH + u
            dst_ref[t] = jnp.broadcast_to(src_ref[pl.ds(t, 1), :], (PEER_PICKS, LANES)).T
        return carry

    lax.fori_loop(0, PEER_TOKEN_TILE // TRANSPOSE_BATCH, batch, 0)


def _expert_row(tab_ref, row8, shift_ref, t, k):
    w = tab_ref[pl.ds(pl.multiple_of(row8, ROW_SUBLANES), ROW_SUBLANES), :]
    s = jnp.broadcast_to(shift_ref[t, pl.ds(k, 1), :], (ROW_SUBLANES, LANES)).astype(jnp.uint32)
    return lax.bitcast_convert_type((w << s) & jnp.uint32(HIGH_HALF_MASK), jnp.float32)


def _fold_pair(a, b, shift, keep_a):
    if 2 * shift == ROW_SUBLANES:
        return jnp.where(keep_a, a, b) + pltpu.roll(jnp.where(keep_a, b, a), shift, axis=0)
    return (jnp.where(keep_a, a, pltpu.roll(b, shift, axis=0))
            + jnp.where(keep_a, pltpu.roll(a, ROW_SUBLANES - shift, axis=0), b))


def _sublane_sums(tiles, masks):
    level = [tiles[_BITREV8[s]] for s in range(ROW_SUBLANES)]
    for shift, keep in masks:
        level = [_fold_pair(level[2 * i], level[2 * i + 1], shift, keep)
                 for i in range(len(level) // 2)]
    return level[0]


def _peer_hidden_body(row_ref, shift_ref, x_ref, gate_ref, utab_ref, act_ref,
                      part_ref, sh_ref, s_ref):
    sub = lax.broadcasted_iota(jnp.int32, (ROW_SUBLANES, LANES), 0)
    masks = [(4, (sub & 4) == 0), (2, (sub & 2) == 0), (1, (sub & 1) == 0)]

    _lane_replicated(shift_ref, sh_ref)

    def token(t, carry):
        x = x_ref[t]
        for g in range(PEER_PICKS // ROW_SUBLANES):
            prods = [_expert_row(utab_ref, row_ref[t, g * ROW_SUBLANES + j], sh_ref, t,
                                 g * ROW_SUBLANES + j) * x
                     for j in range(ROW_SUBLANES)]
            part_ref[t, pl.ds(g * ROW_SUBLANES, ROW_SUBLANES), :] = _sublane_sums(prods, masks)
        return carry

    lax.fori_loop(0, PEER_TOKEN_TILE, token, 0)

    def lane_sums(i, carry):
        for u in range(TRANSPOSE_BATCH):
            t = i * TRANSPOSE_BATCH + u
            s_ref[pl.ds(t, 1), :] = jnp.sum(part_ref[t].T, axis=0, keepdims=True)
        return carry

    lax.fori_loop(0, PEER_TOKEN_TILE // TRANSPOSE_BATCH, lane_sums, 0)
    act_ref[...] = _gelu_exact(s_ref[...]) * gate_ref[...]


def _peer_out_body(row_ref, shift_ref, act_ref, vtab_ref, y_ref, abc_ref, sh_ref):
    _lane_replicated(act_ref, abc_ref)
    _lane_replicated(shift_ref, sh_ref)

    def token(t, carry):
        acc = jnp.zeros((ROW_SUBLANES, LANES), jnp.float32)
        for k in range(PEER_PICKS):
            r = _expert_row(vtab_ref, row_ref[t, k], sh_ref, t, k)
            acc = acc + jnp.broadcast_to(abc_ref[t, pl.ds(k, 1), :], (ROW_SUBLANES, LANES)) * r
        y_ref[t] = acc
        return carry

    lax.fori_loop(0, PEER_TOKEN_TILE, token, 0)


def _peer_experts(xn, rows8, shifts, gates, expert_u, expert_v):
    T, d = xn.shape
    tt = PEER_TOKEN_TILE
    assert T % tt == 0 and d == ROW_SUBLANES * LANES and ROW_SUBLANES == VREG_SUBLANES
    x3 = xn.reshape(T, ROW_SUBLANES, LANES)
    smem_spec = pl.BlockSpec((tt, PEER_PICKS), lambda i: (i, 0), memory_space=pltpu.SMEM)
    pick_spec = pl.BlockSpec((tt, PEER_PICKS), lambda i: (i, 0))
    tok_spec = pl.BlockSpec((tt, ROW_SUBLANES, LANES), lambda i: (i, 0, 0))
    table_spec = pl.BlockSpec(memory_space=pltpu.VMEM)
    params = pltpu.CompilerParams(dimension_semantics=("arbitrary",),
                                  vmem_limit_bytes=PEER_VMEM_LIMIT)
    act = pl.pallas_call(
        _peer_hidden_body,
        grid=(T // tt,),
        in_specs=[smem_spec, pick_spec, tok_spec, pick_spec, table_spec],
        out_specs=pick_spec,
        out_shape=jax.ShapeDtypeStruct((T, PEER_PICKS), jnp.float32),
        scratch_shapes=[pltpu.VMEM((tt, PEER_PICKS, LANES), jnp.float32),
                        pltpu.VMEM((tt, PEER_PICKS, LANES), jnp.int32),
                        pltpu.VMEM((tt, PEER_PICKS), jnp.float32)],
        compiler_params=params,
        name="peer_hidden",
    )(rows8, shifts, x3, gates, _pack_expert_table(expert_u))
    y = pl.pallas_call(
        _peer_out_body,
        grid=(T // tt,),
        in_specs=[smem_spec, pick_spec, pick_spec, table_spec],
        out_specs=tok_spec,
        out_shape=jax.ShapeDtypeStruct((T, ROW_SUBLANES, LANES), jnp.float32),
        scratch_shapes=[pltpu.VMEM((tt, PEER_PICKS, LANES), jnp.float32),
                        pltpu.VMEM((tt, PEER_PICKS, LANES), jnp.int32)],
        compiler_params=params,
        name="peer_out",
    )(rows8, shifts, act, _pack_expert_table(expert_v))
    return y.reshape(T, d)


def _peer_ffn(h, g_ffn, w_q, sub_keys, expert_u, expert_v):
    xn, rows8, shifts, gates = _peer_route(h, g_ffn, w_q, sub_keys)
    return _peer_experts(xn, rows8, shifts, gates, expert_u, expert_v)


def _final_norm_body(h_ref, g_ref, o_ref):
    xf = h_ref[...]
    ms = jnp.mean(xf * xf, axis=-1, keepdims=True)
    o_ref[...] = xf * lax.rsqrt(ms + EPS) * g_ref[...]


def _final_norm(h, g):
    rows, d = h.shape
    tile = 512
    return pl.pallas_call(
        _final_norm_body,
        grid=(rows // tile,),
        in_specs=[pl.BlockSpec((tile, d), lambda i: (i, 0)),
                  pl.BlockSpec((1, d), lambda i: (0, 0))],
        out_specs=pl.BlockSpec((tile, d), lambda i: (i, 0)),
        out_shape=jax.ShapeDtypeStruct((rows, d), jnp.float32),
        name="final_norm",
    )(h, g.reshape(1, d))


def kernel(x, meta_tokens, g_mix, w_in, conv_ssd_w, conv_ssd_b, dt_bias, a_log, d_skip,
           ssd_norm_w, conv_sc_w, w_out, g_ffn, w_q, sub_keys, expert_u, expert_v, g_final):
    bsz, seq, d = x.shape
    meta = jnp.broadcast_to(meta_tokens[None].astype(x.dtype), (bsz, N_META, d))
    h = jnp.concatenate([meta, x], axis=1)
    L = h.shape[1]
    for l in range(g_mix.shape[0]):
        h = h + _hybrid_mixer(_rmsnorm(h, g_mix[l]), w_in[l], conv_ssd_w[l], conv_ssd_b[l],
                              dt_bias[l], a_log[l], d_skip[l], ssd_norm_w[l], conv_sc_w[l],
                              w_out[l])
        ffn = _peer_ffn(h.reshape(bsz * L, d), g_ffn[l], w_q[l], sub_keys[l],
                        expert_u[l], expert_v[l])
        h = h + ffn.reshape(bsz, L, d)
    out = _final_norm(h[:, N_META:].reshape(bsz * seq, d), g_final)
    return out.reshape(bsz, seq, d)
```

```python
import numpy as np

import jax
import jax.numpy as jnp
from jax import lax
from jax.experimental import pallas as pl
from jax.experimental.pallas import tpu as pltpu

D_MODEL = 1024
N_META = 16
D_MIX = 2 * D_MODEL
SSD_HEAD_DIM = 64
SSD_HEADS = 24
D_SSD = SSD_HEADS * SSD_HEAD_DIM
SSD_GROUPS = 4
SSD_STATE = 128
SSD_CONV = 4
SSD_CHUNK = 128
D_XBC = D_SSD + 2 * SSD_GROUPS * SSD_STATE
D_SC = D_MIX - D_SSD
SC_CONV = 3
D_IN = D_SSD + D_XBC + SSD_HEADS + 3 * D_SC
PEER_HEADS = 8
PEER_KEYS = 128
PEER_TOPK = 16
PEER_QDIM = 256
PEER_HALF = PEER_QDIM // 2
EPS = 1e-6

LANES = 128
VREG_SUBLANES = 8


SSD_PAD = SSD_CHUNK - N_META
HEADS_PER_GROUP = SSD_HEADS // SSD_GROUPS
D_GROUP = HEADS_PER_GROUP * SSD_HEAD_DIM
D_BC = SSD_GROUPS * SSD_STATE
D_SC3 = 3 * D_SC
D_IN_ALIGNED = D_SSD + D_XBC + LANES + D_SC3
INPROJ_TOKEN_TILE = 256
INPROJ_COL_TILE = 512
HALO_ROWS = VREG_SUBLANES


def _inproj_body(h_ref, g_ref, w_ref, z_ref, xbc_ref, dt_ref, sc_ref):
    hx = h_ref[...]
    xn = hx * lax.rsqrt(jnp.mean(hx * hx, axis=-1, keepdims=True) + EPS) * g_ref[...]
    xb = xn.astype(jnp.bfloat16)
    col = 0
    for out_ref in (z_ref, xbc_ref, dt_ref, sc_ref):
        width = out_ref.shape[1]
        for c0 in range(0, width, INPROJ_COL_TILE):
            cw = min(INPROJ_COL_TILE, width - c0)
            out_ref[:, c0:c0 + cw] = jnp.dot(xb, w_ref[:, col + c0:col + c0 + cw],
                                             preferred_element_type=jnp.float32)
        col += width


def _inproj(h_pad, g_mix, w_in):
    rows, d = h_pad.shape
    tm = INPROJ_TOKEN_TILE
    assert rows % tm == 0
    s1, s2, s3 = D_SSD, D_SSD + D_XBC, D_SSD + D_XBC + SSD_HEADS
    w = jnp.concatenate([w_in[:, :s2], w_in[:, s2:s3],
                         jnp.zeros((d, LANES - SSD_HEADS), w_in.dtype), w_in[:, s3:]],
                        axis=1).astype(jnp.bfloat16)
    widths = (D_SSD, D_XBC, LANES, D_SC3)
    return pl.pallas_call(
        _inproj_body,
        grid=(rows // tm,),
        in_specs=[pl.BlockSpec((tm, d), lambda i: (i, 0)),
                  pl.BlockSpec((1, d), lambda i: (0, 0)),
                  pl.BlockSpec((d, D_IN_ALIGNED), lambda i: (0, 0))],
        out_specs=[pl.BlockSpec((tm, wd), lambda i: (i, 0)) for wd in widths],
        out_shape=[jax.ShapeDtypeStruct((rows, wd), jnp.float32) for wd in widths],
        compiler_params=pltpu.CompilerParams(dimension_semantics=("arbitrary",)),
        name="mixer_inproj",
    )(h_pad, g_mix.reshape(1, d), w)


def _silu(v):
    return v / (1.0 + jnp.exp(-v))


def _mixer_chunk_body(z_ref, xbc_ref, dt_ref, sc_ref, x_ref, cw_ref, cb_ref, dtb_ref, alog_ref,
                      dskip_ref, nw_ref, scw_ref, expand_ref, wout_ref, o_ref,
                      ext_ref, ext2_ref, state_ref):
    c = pl.program_id(1)
    q = SSD_CHUNK
    hi = lax.Precision.HIGHEST

    @pl.when(c == 0)
    def _():
        ext_ref[0:HALO_ROWS, :] = jnp.zeros((HALO_ROWS, D_XBC), jnp.float32)
        ext2_ref[0:HALO_ROWS, :] = jnp.zeros((HALO_ROWS, D_SC), jnp.float32)
        state_ref[...] = jnp.zeros(state_ref.shape, jnp.float32)

    row = lax.broadcasted_iota(jnp.int32, (q, 1), 0)
    live = jnp.logical_or(c > 0, row >= SSD_PAD)

    ext_ref[HALO_ROWS:HALO_ROWS + q, :] = xbc_ref[...]
    conv = jnp.broadcast_to(cb_ref[...], (q, D_XBC))
    for j in range(SSD_CONV):
        conv = conv + cw_ref[j:j + 1, :] * ext_ref[pl.ds(HALO_ROWS - (SSD_CONV - 1) + j, q), :]
    ext_ref[0:HALO_ROWS, :] = ext_ref[q:q + HALO_ROWS, :]
    xbc = jnp.where(live, _silu(conv), 0.0)
    xs = xbc[:, :D_SSD]
    bm = xbc[:, D_SSD:D_SSD + D_BC]
    cm = xbc[:, D_SSD + D_BC:]

    dtr = dt_ref[...] + dtb_ref[...]
    dt = jnp.maximum(dtr, 0.0) + jnp.log1p(jnp.exp(-jnp.abs(dtr)))
    dt = jnp.where(live, dt, 0.0)
    a = dt * (-jnp.exp(alog_ref[...]))
    tri = (lax.broadcasted_iota(jnp.int32, (q, q), 0)
           >= lax.broadcasted_iota(jnp.int32, (q, q), 1))
    a_cs = jnp.dot(tri.astype(jnp.float32), a, precision=hi,
                   preferred_element_type=jnp.float32)
    a_cs_t = a_cs.T
    a_last = a_cs[q - 1:q, :]
    expand = expand_ref[...]
    dt_w = jnp.dot(dt, expand, precision=hi, preferred_element_type=jnp.float32)
    acs_w = jnp.dot(a_cs, expand, precision=hi, preferred_element_type=jnp.float32)
    alast_w = jnp.dot(a_last, expand, precision=hi, preferred_element_type=jnp.float32)
    x_dt = xs * dt_w
    grow = jnp.exp(acs_w)
    x_tail = (x_dt * jnp.exp(alast_w - acs_w)).astype(jnp.bfloat16)
    x_dt_b = x_dt.astype(jnp.bfloat16)
    lane = lax.broadcasted_iota(jnp.int32, (q, LANES), 1)

    y_groups = []
    for g in range(SSD_GROUPS):
        gs = slice(g * D_GROUP, (g + 1) * D_GROUP)
        bg = bm[:, g * SSD_STATE:(g + 1) * SSD_STATE]
        cg = cm[:, g * SSD_STATE:(g + 1) * SSD_STATE].astype(jnp.bfloat16)
        cb = lax.dot_general(cg, bg.astype(jnp.bfloat16), (((1,), (1,)), ((), ())),
                             preferred_element_type=jnp.float32)
        state = state_ref[g]
        y_off = jnp.dot(cg, state.astype(jnp.bfloat16),
                        preferred_element_type=jnp.float32) * grow[:, gs]
        y_pairs = []
        for pair in range(HEADS_PER_GROUP // 2):
            cols = slice(g * D_GROUP + pair * LANES, g * D_GROUP + (pair + 1) * LANES)
            xp = x_dt_b[:, cols]
            halves = []
            for sub in range(2):
                hd = g * HEADS_PER_GROUP + 2 * pair + sub
                seg = (jnp.broadcast_to(a_cs[:, hd:hd + 1], (q, q))
                       - jnp.broadcast_to(a_cs_t[hd:hd + 1, :], (q, q)))
                decay = jnp.exp(jnp.where(tri, seg, -jnp.inf))
                halves.append(jnp.dot((cb * decay).astype(jnp.bfloat16), xp,
                                      preferred_element_type=jnp.float32))
            y_pairs.append(jnp.where(lane < SSD_HEAD_DIM, halves[0], halves[1]))
        y_groups.append(jnp.concatenate(y_pairs, axis=1) + y_off)
        bg_t = bg.T.astype(jnp.bfloat16)
        state_ref[g] = (state * jnp.exp(alast_w[:, gs])
                        + jnp.dot(bg_t, x_tail[:, gs], preferred_element_type=jnp.float32))

    y = jnp.concatenate(y_groups, axis=1) + dskip_ref[...] * xs
    y = y * _silu(z_ref[...])
    normed = []
    for g in range(SSD_GROUPS):
        yg = y[:, g * D_GROUP:(g + 1) * D_GROUP]
        ms = jnp.mean(yg * yg, axis=-1, keepdims=True)
        normed.append(yg * lax.rsqrt(ms + EPS))
    y_ssd = jnp.concatenate(normed, axis=1) * nw_ref[...]

    sc = sc_ref[...]
    ext2_ref[HALO_ROWS:HALO_ROWS + q, :] = sc[:, D_SC:2 * D_SC] * sc[:, 2 * D_SC:]
    conv2 = jnp.zeros((q, D_SC), jnp.float32)
    for j in range(SC_CONV):
        conv2 = conv2 + scw_ref[j:j + 1, :] * ext2_ref[pl.ds(HALO_ROWS - (SC_CONV - 1) + j, q), :]
    ext2_ref[0:HALO_ROWS, :] = ext2_ref[q:q + HALO_ROWS, :]
    y_sc = sc[:, :D_SC] * conv2

    @pl.when(c > 0)
    def _():
        ycat = jnp.concatenate([y_ssd, y_sc], axis=1).astype(jnp.bfloat16)
        o_ref[...] = x_ref[...] + jnp.dot(ycat, wout_ref[...],
                                          preferred_element_type=jnp.float32)


def _head_expand_matrix():
    e = np.zeros((LANES, D_SSD), np.float32)
    for hd in range(SSD_HEADS):
        e[hd, hd * SSD_HEAD_DIM:(hd + 1) * SSD_HEAD_DIM] = 1.0
    return jnp.asarray(e)


def _mixer(x, meta_tokens, g_mix, w_in, conv_ssd_w, conv_ssd_b, dt_bias, a_log, d_skip,
           ssd_norm_w, conv_sc_w, w_out):
    bsz, seq, d = x.shape
    assert seq % SSD_CHUNK == 0
    n_chunks = seq // SSD_CHUNK + 1
    meta = jnp.broadcast_to(meta_tokens[None].astype(x.dtype), (bsz, N_META, d))
    h_pad = jnp.concatenate([jnp.zeros((bsz, SSD_PAD, d), x.dtype), meta, x], axis=1)
    z, xbc, dt_raw, sc = _inproj(h_pad.reshape(bsz * n_chunks * SSD_CHUNK, d), g_mix, w_in)

    def pad_heads(v):
        return jnp.pad(v.astype(jnp.float32), (0, LANES - SSD_HEADS)).reshape(1, LANES)

    params = [conv_ssd_w, conv_ssd_b.reshape(1, D_XBC), pad_heads(dt_bias), pad_heads(a_log),
              jnp.repeat(d_skip.astype(jnp.float32), SSD_HEAD_DIM).reshape(1, D_SSD),
              ssd_norm_w.reshape(1, D_SSD), conv_sc_w, _head_expand_matrix(),
              w_out.astype(jnp.bfloat16)]

    def chunk_rows(width):
        return pl.BlockSpec((SSD_CHUNK, width), lambda b, c: (b * n_chunks + c, 0))

    def seq_rows(b, c):
        return (b * (n_chunks - 1) + jnp.maximum(c - 1, 0), 0)

    def whole(arr):
        return pl.BlockSpec(arr.shape, lambda b, c: (0,) * arr.ndim)

    return pl.pallas_call(
        _mixer_chunk_body,
        grid=(bsz, n_chunks),
        in_specs=[chunk_rows(D_SSD), chunk_rows(D_XBC), chunk_rows(LANES), chunk_rows(D_SC3),
                  pl.BlockSpec((SSD_CHUNK, d), seq_rows)] + [whole(p) for p in params],
        out_specs=pl.BlockSpec((SSD_CHUNK, d), seq_rows),
        out_shape=jax.ShapeDtypeStruct((bsz * seq, d), jnp.float32),
        scratch_shapes=[pltpu.VMEM((HALO_ROWS + SSD_CHUNK, D_XBC), jnp.float32),
                        pltpu.VMEM((HALO_ROWS + SSD_CHUNK, D_SC), jnp.float32),
                        pltpu.VMEM((SSD_GROUPS, SSD_STATE, D_GROUP), jnp.float32)],
        compiler_params=pltpu.CompilerParams(dimension_semantics=("arbitrary", "arbitrary")),
        name="mixer_chunk",
    )(z, xbc, dt_raw, sc, x.reshape(bsz * seq, d), *params)


PEER_PICKS = PEER_HEADS * PEER_TOPK
PACKED_ROWS = PEER_KEYS * PEER_KEYS // 2
ROUTE_TOKEN_TILE = 256

_STAIR = [(a, b) for a in range(PEER_TOPK) for b in range(PEER_TOPK)
          if (a + 1) * (b + 1) <= PEER_TOPK]
STAIR_ROWS = -(-len(_STAIR) // VREG_SUBLANES) * VREG_SUBLANES


def _stair_constants():
    sel_a = np.zeros((STAIR_ROWS, PEER_TOPK), np.float32)
    sel_b = np.zeros((STAIR_ROWS, PEER_TOPK), np.float32)
    pad = np.full((STAIR_ROWS, LANES), -np.inf, np.float32)
    for c, (a, b) in enumerate(_STAIR):
        sel_a[c, a] = 1.0
        sel_b[c, b] = 1.0
        pad[c, :] = 0.0
    return jnp.asarray(sel_a), jnp.asarray(sel_b), jnp.asarray(pad)


def _top16_rows(c, val_ref, idx_ref):
    rows = c.shape[0]
    iota = lax.broadcasted_iota(jnp.int32, c.shape, 0).astype(jnp.float32)
    for r in range(PEER_TOPK):
        m = jnp.max(c, axis=0, keepdims=True)
        idx = jnp.min(jnp.where(c == m, iota, float(rows)), axis=0, keepdims=True)
        val_ref[pl.ds(r, 1), :] = m
        idx_ref[pl.ds(r, 1), :] = idx
        c = jnp.where(iota == idx, -jnp.inf, c)


def _peer_route_body(h_ref, g_ref, wq_ref, keys_ref, sela_ref, selb_ref, pad_ref,
                     xn_ref, rows_ref, shift_ref, gate_ref,
                     q_ref, v_ref, i_ref, best_ref, pos_ref, e_ref, p_ref):
    hx = h_ref[...]
    xn = hx * lax.rsqrt(jnp.mean(hx * hx, axis=-1, keepdims=True) + EPS) * g_ref[...]
    xn_ref[...] = xn
    xb = xn.astype(jnp.bfloat16)
    for hh in range(PEER_HEADS):
        qh = jnp.dot(xb, wq_ref[:, hh * PEER_QDIM:(hh + 1) * PEER_QDIM],
                     preferred_element_type=jnp.float32).astype(jnp.bfloat16)
        q_ref[2 * hh] = qh[:, :PEER_HALF]
        q_ref[2 * hh + 1] = qh[:, PEER_HALF:]

    for lt in range(ROUTE_TOKEN_TILE // LANES):
        tok = pl.ds(lt * LANES, LANES)

        def head(hh, carry):
            for half in range(2):
                j = 2 * hh + half
                s_t = lax.dot_general(keys_ref[j], q_ref[j, tok, :], (((1,), (1,)), ((), ())),
                                      preferred_element_type=jnp.float32)
                _top16_rows(s_t, v_ref.at[half], i_ref.at[half])
            v1, v2 = v_ref[0], v_ref[1]
            i1, i2 = i_ref[0], i_ref[1]
            hi = lax.Precision.HIGHEST
            cand = (jnp.dot(sela_ref[...], v1, precision=hi, preferred_element_type=jnp.float32)
                    + jnp.dot(selb_ref[...], v2, precision=hi, preferred_element_type=jnp.float32)
                    + pad_ref[...])
            ea = jnp.dot(sela_ref[...], i1, preferred_element_type=jnp.float32)
            eb = jnp.dot(selb_ref[...], i2, preferred_element_type=jnp.float32)
            cand_e = (ea * PEER_KEYS + eb).astype(jnp.int32)
            _top16_rows(cand, best_ref, pos_ref)
            best = best_ref[...]
            pos = pos_ref[...]
            crow = lax.broadcasted_iota(jnp.int32, cand.shape, 0).astype(jnp.float32)
            picks = []
            for r in range(PEER_TOPK):
                hit = crow == pos[r:r + 1, :]
                picks.append(jnp.sum(jnp.where(hit, cand_e, 0), axis=0, keepdims=True))
            w = jnp.exp(best - best[0:1, :])
            gates = w / jnp.sum(w, axis=0, keepdims=True)
            base = pl.multiple_of(hh * PEER_TOPK, PEER_TOPK)
            p_ref[pl.ds(base, PEER_TOPK), :] = gates
            e_ref[pl.ds(base, PEER_TOPK), :] = jnp.concatenate(picks, axis=0)
            return carry

        lax.fori_loop(0, PEER_HEADS, head, 0)
        e_t = e_ref[...].T
        rows_ref[tok, :] = (e_t & (PACKED_ROWS - 1)) * VREG_SUBLANES
        shift_ref[tok, :] = (1 - (e_t >> 13)) * 16
        gate_ref[tok, :] = p_ref[...].T


def _peer_route(h, g_ffn, w_q, sub_keys):
    T, d = h.shape
    tt = ROUTE_TOKEN_TILE
    assert T % tt == 0 and PACKED_ROWS == 1 << 13
    sel_a, sel_b, pad = _stair_constants()
    keys = sub_keys.reshape(PEER_HEADS * 2, PEER_KEYS, PEER_HALF).astype(jnp.bfloat16)
    tok_spec = pl.BlockSpec((tt, d), lambda i: (i, 0))
    pick_spec = pl.BlockSpec((tt, PEER_PICKS), lambda i: (i, 0))

    def whole(shape):
        return pl.BlockSpec(shape, lambda i: (0,) * len(shape))

    return pl.pallas_call(
        _peer_route_body,
        grid=(T // tt,),
        in_specs=[tok_spec, whole((1, d)), whole((d, PEER_HEADS * PEER_QDIM)),
                  whole(keys.shape), whole(sel_a.shape), whole(sel_b.shape), whole(pad.shape)],
        out_specs=[tok_spec, pick_spec, pick_spec, pick_spec],
        out_shape=[jax.ShapeDtypeStruct((T, d), jnp.float32),
                   jax.ShapeDtypeStruct((T, PEER_PICKS), jnp.int32),
                   jax.ShapeDtypeStruct((T, PEER_PICKS), jnp.int32),
                   jax.ShapeDtypeStruct((T, PEER_PICKS), jnp.float32)],
        scratch_shapes=[pltpu.VMEM((2 * PEER_HEADS, tt, PEER_HALF), jnp.bfloat16),
                        pltpu.VMEM((2, PEER_TOPK, LANES), jnp.float32),
                        pltpu.VMEM((2, PEER_TOPK, LANES), jnp.float32),
                        pltpu.VMEM((PEER_TOPK, LANES), jnp.float32),
                        pltpu.VMEM((PEER_TOPK, LANES), jnp.float32),
                        pltpu.VMEM((PEER_PICKS, LANES), jnp.int32),
                        pltpu.VMEM((PEER_PICKS, LANES), jnp.float32)],
        compiler_params=pltpu.CompilerParams(dimension_semantics=("arbitrary",)),
        name="peer_route",
    )(h, g_ffn.reshape(1, d), w_q.astype(jnp.bfloat16), keys, sel_a, sel_b, pad)


ROW_SUBLANES = D_MODEL // LANES
PEER_TOKEN_TILE = 64
PACKED_TABLE_BYTES = PACKED_ROWS * D_MODEL * 4
LANE_REPLICATED_BYTES = PEER_TOKEN_TILE * PEER_PICKS * LANES * 4
PEER_VMEM_LIMIT = PACKED_TABLE_BYTES + 2 * LANE_REPLICATED_BYTES + 8 * 1024 * 1024
HIGH_HALF_MASK = 0xFFFF0000
_BITREV8 = (0, 4, 2, 6, 1, 5, 3, 7)


def _pack_expert_table(tab):
    bits = lax.bitcast_convert_type(tab.astype(jnp.bfloat16), jnp.uint16).astype(jnp.uint32)
    packed = bits[:PACKED_ROWS] | (bits[PACKED_ROWS:] << 16)
    return packed.reshape(PACKED_ROWS * ROW_SUBLANES, LANES)


def _gelu_exact(s):
    return 0.5 * s * (1.0 + lax.erf(s * (2.0 ** -0.5)))


TRANSPOSE_BATCH = 8
OUT_PARTIAL_SUMS = 4


def _lane_replicated(src_ref, dst_ref):
    def batch(i, carry):
        for u in range(TRANSPOSE_BATCH):
            t = i * TRANSPOSE_BATCH + u
            dst_ref[t] = jnp.broadcast_to(src_ref[pl.ds(t, 1), :], (PEER_PICKS, LANES)).T
        return carry

    lax.fori_loop(0, PEER_TOKEN_TILE // TRANSPOSE_BATCH, batch, 0)


def _expert_row(tab_ref, row8, shift_ref, t, k):
    w = tab_ref[pl.ds(pl.multiple_of(row8, ROW_SUBLANES), ROW_SUBLANES), :]
    s = jnp.broadcast_to(shift_ref[t, pl.ds(k, 1), :], (ROW_SUBLANES, LANES)).astype(jnp.uint32)
    return lax.bitcast_convert_type((w << s) & jnp.uint32(HIGH_HALF_MASK), jnp.float32)


def _fold_pair(a, b, shift, keep_a):
    if 2 * shift == ROW_SUBLANES:
        return jnp.where(keep_a, a, b) + pltpu.roll(jnp.where(keep_a, b, a), shift, axis=0)
    return (jnp.where(keep_a, a, pltpu.roll(b, shift, axis=0))
            + jnp.where(keep_a, pltpu.roll(a, ROW_SUBLANES - shift, axis=0), b))


def _sublane_sums(tiles, masks):
    level = [tiles[_BITREV8[s]] for s in range(ROW_SUBLANES)]
    for shift, keep in masks:
        level = [_fold_pair(level[2 * i], level[2 * i + 1], shift, keep)
                 for i in range(len(level) // 2)]
    return level[0]


def _peer_hidden_body(row_ref, shift_ref, x_ref, gate_ref, utab_ref, act_ref,
                      part_ref, sh_ref, s_ref):
    sub = lax.broadcasted_iota(jnp.int32, (ROW_SUBLANES, LANES), 0)
    masks = [(4, (sub & 4) == 0), (2, (sub & 2) == 0), (1, (sub & 1) == 0)]

    _lane_replicated(shift_ref, sh_ref)

    def token(t, carry):
        x = x_ref[t]
        for g in range(PEER_PICKS // ROW_SUBLANES):
            prods = [_expert_row(utab_ref, row_ref[t, g * ROW_SUBLANES + j], sh_ref, t,
                                 g * ROW_SUBLANES + j) * x
                     for j in range(ROW_SUBLANES)]
            part_ref[t, pl.ds(g * ROW_SUBLANES, ROW_SUBLANES), :] = _sublane_sums(prods, masks)
        return carry

    lax.fori_loop(0, PEER_TOKEN_TILE, token, 0)

    def lane_sums(i, carry):
        for u in range(TRANSPOSE_BATCH):
            t = i * TRANSPOSE_BATCH + u
            s_ref[pl.ds(t, 1), :] = jnp.sum(part_ref[t].T, axis=0, keepdims=True)
        return carry

    lax.fori_loop(0, PEER_TOKEN_TILE // TRANSPOSE_BATCH, lane_sums, 0)
    act_ref[...] = _gelu_exact(s_ref[...]) * gate_ref[...]


def _peer_out_body(row_ref, shift_ref, act_ref, vtab_ref, y_ref, abc_ref, sh_ref):
    _lane_replicated(act_ref, abc_ref)
    _lane_replicated(shift_ref, sh_ref)

    def token(t, carry):
        accs = [jnp.zeros((ROW_SUBLANES, LANES), jnp.float32) for _ in range(OUT_PARTIAL_SUMS)]
        for k in range(PEER_PICKS):
            r = _expert_row(vtab_ref, row_ref[t, k], sh_ref, t, k)
            a = jnp.broadcast_to(abc_ref[t, pl.ds(k, 1), :], (ROW_SUBLANES, LANES))
            accs[k % OUT_PARTIAL_SUMS] = accs[k % OUT_PARTIAL_SUMS] + a * r
        y_ref[t] = (accs[0] + accs[1]) + (accs[2] + accs[3])
        return carry

    lax.fori_loop(0, PEER_TOKEN_TILE, token, 0)


def _peer_experts(xn, rows8, shifts, gates, expert_u, expert_v):
    T, d = xn.shape
    tt = PEER_TOKEN_TILE
    assert T % tt == 0 and d == ROW_SUBLANES * LANES and ROW_SUBLANES == VREG_SUBLANES
    x3 = xn.reshape(T, ROW_SUBLANES, LANES)
    smem_spec = pl.BlockSpec((tt, PEER_PICKS), lambda i: (i, 0), memory_space=pltpu.SMEM)
    pick_spec = pl.BlockSpec((tt, PEER_PICKS), lambda i: (i, 0))
    tok_spec = pl.BlockSpec((tt, ROW_SUBLANES, LANES), lambda i: (i, 0, 0))
    table_spec = pl.BlockSpec(memory_space=pltpu.VMEM)
    params = pltpu.CompilerParams(dimension_semantics=("arbitrary",),
                                  vmem_limit_bytes=PEER_VMEM_LIMIT)
    act = pl.pallas_call(
        _peer_hidden_body,
        grid=(T // tt,),
        in_specs=[smem_spec, pick_spec, tok_spec, pick_spec, table_spec],
        out_specs=pick_spec,
        out_shape=jax.ShapeDtypeStruct((T, PEER_PICKS), jnp.float32),
        scratch_shapes=[pltpu.VMEM((tt, PEER_PICKS, LANES), jnp.float32),
                        pltpu.VMEM((tt, PEER_PICKS, LANES), jnp.int32),
                        pltpu.VMEM((tt, PEER_PICKS), jnp.float32)],
        compiler_params=params,
        name="peer_hidden",
    )(rows8, shifts, x3, gates, _pack_expert_table(expert_u))
    y = pl.pallas_call(
        _peer_out_body,
        grid=(T // tt,),
        in_specs=[smem_spec, pick_spec, pick_spec, table_spec],
        out_specs=tok_spec,
        out_shape=jax.ShapeDtypeStruct((T, ROW_SUBLANES, LANES), jnp.float32),
        scratch_shapes=[pltpu.VMEM((tt, PEER_PICKS, LANES), jnp.float32),
                        pltpu.VMEM((tt, PEER_PICKS, LANES), jnp.int32)],
        compiler_params=params,
        name="peer_out",
    )(rows8, shifts, act, _pack_expert_table(expert_v))
    return y.reshape(T, d)


def _peer_ffn(h, g_ffn, w_q, sub_keys, expert_u, expert_v):
    xn, rows8, shifts, gates = _peer_route(h, g_ffn, w_q, sub_keys)
    return _peer_experts(xn, rows8, shifts, gates, expert_u, expert_v)


def _final_norm_body(h_ref, g_ref, o_ref):
    xf = h_ref[...]
    ms = jnp.mean(xf * xf, axis=-1, keepdims=True)
    o_ref[...] = xf * lax.rsqrt(ms + EPS) * g_ref[...]


def _final_norm(h, g):
    rows, d = h.shape
    tile = 512
    return pl.pallas_call(
        _final_norm_body,
        grid=(rows // tile,),
        in_specs=[pl.BlockSpec((tile, d), lambda i: (i, 0)),
                  pl.BlockSpec((1, d), lambda i: (0, 0))],
        out_specs=pl.BlockSpec((tile, d), lambda i: (i, 0)),
        out_shape=jax.ShapeDtypeStruct((rows, d), jnp.float32),
        name="final_norm",
    )(h, g.reshape(1, d))


def kernel(x, meta_tokens, g_mix, w_in, conv_ssd_w, conv_ssd_b, dt_bias, a_log, d_skip,
           ssd_norm_w, conv_sc_w, w_out, g_ffn, w_q, sub_keys, expert_u, expert_v, g_final):
    bsz, seq, d = x.shape
    assert g_mix.shape[0] == 1
    h = _mixer(x, meta_tokens, g_mix[0], w_in[0], conv_ssd_w[0], conv_ssd_b[0], dt_bias[0],
               a_log[0], d_skip[0], ssd_norm_w[0], conv_sc_w[0], w_out[0])
    h = h + _peer_ffn(h, g_ffn[0], w_q[0], sub_keys[0], expert_u[0], expert_v[0])
    return _final_norm(h, g_final).reshape(bsz, seq, d)
```

```python
import numpy as np

import jax
import jax.numpy as jnp
from jax import lax
from jax.experimental import pallas as pl
from jax.experimental.pallas import tpu as pltpu

D_MODEL = 1024
N_META = 16
D_MIX = 2 * D_MODEL
SSD_HEAD_DIM = 64
SSD_HEADS = 24
D_SSD = SSD_HEADS * SSD_HEAD_DIM
SSD_GROUPS = 4
SSD_STATE = 128
SSD_CONV = 4
SSD_CHUNK = 128
D_XBC = D_SSD + 2 * SSD_GROUPS * SSD_STATE
D_SC = D_MIX - D_SSD
SC_CONV = 3
D_IN = D_SSD + D_XBC + SSD_HEADS + 3 * D_SC
PEER_HEADS = 8
PEER_KEYS = 128
PEER_TOPK = 16
PEER_QDIM = 256
PEER_HALF = PEER_QDIM // 2
EPS = 1e-6

LANES = 128
VREG_SUBLANES = 8


SSD_PAD = SSD_CHUNK - N_META
HEADS_PER_GROUP = SSD_HEADS // SSD_GROUPS
D_GROUP = HEADS_PER_GROUP * SSD_HEAD_DIM
D_BC = SSD_GROUPS * SSD_STATE
D_SC3 = 3 * D_SC
D_IN_ALIGNED = D_SSD + D_XBC + LANES + D_SC3
INPROJ_TOKEN_TILE = 256
INPROJ_COL_TILE = 512
HALO_ROWS = VREG_SUBLANES


def _inproj_body(h_ref, g_ref, w_ref, z_ref, xbc_ref, dt_ref, sc_ref):
    hx = h_ref[...]
    xn = hx * lax.rsqrt(jnp.mean(hx * hx, axis=-1, keepdims=True) + EPS) * g_ref[...]
    xb = xn.astype(jnp.bfloat16)
    col = 0
    for out_ref in (z_ref, xbc_ref, dt_ref, sc_ref):
        width = out_ref.shape[1]
        for c0 in range(0, width, INPROJ_COL_TILE):
            cw = min(INPROJ_COL_TILE, width - c0)
            out_ref[:, c0:c0 + cw] = jnp.dot(xb, w_ref[:, col + c0:col + c0 + cw],
                                             preferred_element_type=jnp.float32)
        col += width


def _inproj(h_pad, g_mix, w_in):
    rows, d = h_pad.shape
    tm = INPROJ_TOKEN_TILE
    assert rows % tm == 0
    s1, s2, s3 = D_SSD, D_SSD + D_XBC, D_SSD + D_XBC + SSD_HEADS
    w = jnp.concatenate([w_in[:, :s2], w_in[:, s2:s3],
                         jnp.zeros((d, LANES - SSD_HEADS), w_in.dtype), w_in[:, s3:]],
                        axis=1).astype(jnp.bfloat16)
    widths = (D_SSD, D_XBC, LANES, D_SC3)
    return pl.pallas_call(
        _inproj_body,
        grid=(rows // tm,),
        in_specs=[pl.BlockSpec((tm, d), lambda i: (i, 0)),
                  pl.BlockSpec((1, d), lambda i: (0, 0)),
                  pl.BlockSpec((d, D_IN_ALIGNED), lambda i: (0, 0))],
        out_specs=[pl.BlockSpec((tm, wd), lambda i: (i, 0)) for wd in widths],
        out_shape=[jax.ShapeDtypeStruct((rows, wd), jnp.float32) for wd in widths],
        compiler_params=pltpu.CompilerParams(dimension_semantics=("arbitrary",)),
        name="mixer_inproj",
    )(h_pad, g_mix.reshape(1, d), w)


def _silu(v):
    return v / (1.0 + jnp.exp(-v))


def _mixer_chunk_body(z_ref, xbc_ref, dt_ref, sc_ref, x_ref, cw_ref, cb_ref, dtb_ref, alog_ref,
                      dskip_ref, nw_ref, scw_ref, expand_ref, wout_ref, o_ref,
                      ext_ref, ext2_ref, state_ref):
    c = pl.program_id(1)
    q = SSD_CHUNK
    hi = lax.Precision.HIGHEST

    @pl.when(c == 0)
    def _():
        ext_ref[0:HALO_ROWS, :] = jnp.zeros((HALO_ROWS, D_XBC), jnp.float32)
        ext2_ref[0:HALO_ROWS, :] = jnp.zeros((HALO_ROWS, D_SC), jnp.float32)
        state_ref[...] = jnp.zeros(state_ref.shape, jnp.float32)

    row = lax.broadcasted_iota(jnp.int32, (q, 1), 0)
    live = jnp.logical_or(c > 0, row >= SSD_PAD)

    ext_ref[HALO_ROWS:HALO_ROWS + q, :] = xbc_ref[...]
    conv = jnp.broadcast_to(cb_ref[...], (q, D_XBC))
    for j in range(SSD_CONV):
        conv = conv + cw_ref[j:j + 1, :] * ext_ref[pl.ds(HALO_ROWS - (SSD_CONV - 1) + j, q), :]
    ext_ref[0:HALO_ROWS, :] = ext_ref[q:q + HALO_ROWS, :]
    xbc = jnp.where(live, _silu(conv), 0.0)
    xs = xbc[:, :D_SSD]
    bm = xbc[:, D_SSD:D_SSD + D_BC]
    cm = xbc[:, D_SSD + D_BC:]

    dtr = dt_ref[...] + dtb_ref[...]
    dt = jnp.maximum(dtr, 0.0) + jnp.log1p(jnp.exp(-jnp.abs(dtr)))
    dt = jnp.where(live, dt, 0.0)
    a = dt * (-jnp.exp(alog_ref[...]))
    tri = (lax.broadcasted_iota(jnp.int32, (q, q), 0)
           >= lax.broadcasted_iota(jnp.int32, (q, q), 1))
    a_cs = jnp.dot(tri.astype(jnp.float32), a, precision=hi,
                   preferred_element_type=jnp.float32)
    a_cs_t = a_cs.T
    a_last = a_cs[q - 1:q, :]
    expand = expand_ref[...]
    dt_w = jnp.dot(dt, expand, precision=hi, preferred_element_type=jnp.float32)
    acs_w = jnp.dot(a_cs, expand, precision=hi, preferred_element_type=jnp.float32)
    alast_w = jnp.dot(a_last, expand, precision=hi, preferred_element_type=jnp.float32)
    x_dt = xs * dt_w
    grow = jnp.exp(acs_w)
    x_tail = (x_dt * jnp.exp(alast_w - acs_w)).astype(jnp.bfloat16)
    x_dt_b = x_dt.astype(jnp.bfloat16)
    lane = lax.broadcasted_iota(jnp.int32, (q, LANES), 1)

    y_groups = []
    for g in range(SSD_GROUPS):
        gs = slice(g * D_GROUP, (g + 1) * D_GROUP)
        bg = bm[:, g * SSD_STATE:(g + 1) * SSD_STATE]
        cg = cm[:, g * SSD_STATE:(g + 1) * SSD_STATE].astype(jnp.bfloat16)
        cb = lax.dot_general(cg, bg.astype(jnp.bfloat16), (((1,), (1,)), ((), ())),
                             preferred_element_type=jnp.float32)
        state = state_ref[g]
        y_off = jnp.dot(cg, state.astype(jnp.bfloat16),
                        preferred_element_type=jnp.float32) * grow[:, gs]
        y_pairs = []
        for pair in range(HEADS_PER_GROUP // 2):
            cols = slice(g * D_GROUP + pair * LANES, g * D_GROUP + (pair + 1) * LANES)
            xp = x_dt_b[:, cols]
            halves = []
            for sub in range(2):
                hd = g * HEADS_PER_GROUP + 2 * pair + sub
                seg = (jnp.broadcast_to(a_cs[:, hd:hd + 1], (q, q))
                       - jnp.broadcast_to(a_cs_t[hd:hd + 1, :], (q, q)))
                decay = jnp.exp(jnp.where(tri, seg, -jnp.inf))
                halves.append(jnp.dot((cb * decay).astype(jnp.bfloat16), xp,
                                      preferred_element_type=jnp.float32))
            y_pairs.append(jnp.where(lane < SSD_HEAD_DIM, halves[0], halves[1]))
        y_groups.append(jnp.concatenate(y_pairs, axis=1) + y_off)
        bg_t = bg.T.astype(jnp.bfloat16)
        state_ref[g] = (state * jnp.exp(alast_w[:, gs])
                        + jnp.dot(bg_t, x_tail[:, gs], preferred_element_type=jnp.float32))

    y = jnp.concatenate(y_groups, axis=1) + dskip_ref[...] * xs
    y = y * _silu(z_ref[...])
    normed = []
    for g in range(SSD_GROUPS):
        yg = y[:, g * D_GROUP:(g + 1) * D_GROUP]
        ms = jnp.mean(yg * yg, axis=-1, keepdims=True)
        normed.append(yg * lax.rsqrt(ms + EPS))
    y_ssd = jnp.concatenate(normed, axis=1) * nw_ref[...]

    sc = sc_ref[...]
    ext2_ref[HALO_ROWS:HALO_ROWS + q, :] = sc[:, D_SC:2 * D_SC] * sc[:, 2 * D_SC:]
    conv2 = jnp.zeros((q, D_SC), jnp.float32)
    for j in range(SC_CONV):
        conv2 = conv2 + scw_ref[j:j + 1, :] * ext2_ref[pl.ds(HALO_ROWS - (SC_CONV - 1) + j, q), :]
    ext2_ref[0:HALO_ROWS, :] = ext2_ref[q:q + HALO_ROWS, :]
    y_sc = sc[:, :D_SC] * conv2

    @pl.when(c > 0)
    def _():
        ycat = jnp.concatenate([y_ssd, y_sc], axis=1).astype(jnp.bfloat16)
        o_ref[...] = x_ref[...] + jnp.dot(ycat, wout_ref[...],
                                          preferred_element_type=jnp.float32)


def _head_expand_matrix():
    e = np.zeros((LANES, D_SSD), np.float32)
    for hd in range(SSD_HEADS):
        e[hd, hd * SSD_HEAD_DIM:(hd + 1) * SSD_HEAD_DIM] = 1.0
    return jnp.asarray(e)


def _mixer(x, meta_tokens, g_mix, w_in, conv_ssd_w, conv_ssd_b, dt_bias, a_log, d_skip,
           ssd_norm_w, conv_sc_w, w_out):
    bsz, seq, d = x.shape
    assert seq % SSD_CHUNK == 0
    n_chunks = seq // SSD_CHUNK + 1
    meta = jnp.broadcast_to(meta_tokens[None].astype(x.dtype), (bsz, N_META, d))
    h_pad = jnp.concatenate([jnp.zeros((bsz, SSD_PAD, d), x.dtype), meta, x], axis=1)
    z, xbc, dt_raw, sc = _inproj(h_pad.reshape(bsz * n_chunks * SSD_CHUNK, d), g_mix, w_in)

    def pad_heads(v):
        return jnp.pad(v.astype(jnp.float32), (0, LANES - SSD_HEADS)).reshape(1, LANES)

    params = [conv_ssd_w, conv_ssd_b.reshape(1, D_XBC), pad_heads(dt_bias), pad_heads(a_log),
              jnp.repeat(d_skip.astype(jnp.float32), SSD_HEAD_DIM).reshape(1, D_SSD),
              ssd_norm_w.reshape(1, D_SSD), conv_sc_w, _head_expand_matrix(),
              w_out.astype(jnp.bfloat16)]

    def chunk_rows(width):
        return pl.BlockSpec((SSD_CHUNK, width), lambda b, c: (b * n_chunks + c, 0))

    def seq_rows(b, c):
        return (b * (n_chunks - 1) + jnp.maximum(c - 1, 0), 0)

    def whole(arr):
        return pl.BlockSpec(arr.shape, lambda b, c: (0,) * arr.ndim)

    return pl.pallas_call(
        _mixer_chunk_body,
        grid=(bsz, n_chunks),
        in_specs=[chunk_rows(D_SSD), chunk_rows(D_XBC), chunk_rows(LANES), chunk_rows(D_SC3),
                  pl.BlockSpec((SSD_CHUNK, d), seq_rows)] + [whole(p) for p in params],
        out_specs=pl.BlockSpec((SSD_CHUNK, d), seq_rows),
        out_shape=jax.ShapeDtypeStruct((bsz * seq, d), jnp.float32),
        scratch_shapes=[pltpu.VMEM((HALO_ROWS + SSD_CHUNK, D_XBC), jnp.float32),
                        pltpu.VMEM((HALO_ROWS + SSD_CHUNK, D_SC), jnp.float32),
                        pltpu.VMEM((SSD_GROUPS, SSD_STATE, D_GROUP), jnp.float32)],
        compiler_params=pltpu.CompilerParams(dimension_semantics=("arbitrary", "arbitrary")),
        name="mixer_chunk",
    )(z, xbc, dt_raw, sc, x.reshape(bsz * seq, d), *params)


PEER_PICKS = PEER_HEADS * PEER_TOPK
PACKED_ROWS = PEER_KEYS * PEER_KEYS // 2
ROUTE_TOKEN_TILE = 256

_STAIR = [(a, b) for a in range(PEER_TOPK) for b in range(PEER_TOPK)
          if (a + 1) * (b + 1) <= PEER_TOPK]
STAIR_ROWS = -(-len(_STAIR) // VREG_SUBLANES) * VREG_SUBLANES


def _stair_constants():
    sel_a = np.zeros((STAIR_ROWS, PEER_TOPK), np.float32)
    sel_b = np.zeros((STAIR_ROWS, PEER_TOPK), np.float32)
    pad = np.full((STAIR_ROWS, LANES), -np.inf, np.float32)
    for c, (a, b) in enumerate(_STAIR):
        sel_a[c, a] = 1.0
        sel_b[c, b] = 1.0
        pad[c, :] = 0.0
    return jnp.asarray(sel_a), jnp.asarray(sel_b), jnp.asarray(pad)


def _top16_rows(c, val_ref, idx_ref):
    rows = c.shape[0]
    iota = lax.broadcasted_iota(jnp.int32, c.shape, 0).astype(jnp.float32)
    for r in range(PEER_TOPK):
        m = jnp.max(c, axis=0, keepdims=True)
        idx = jnp.min(jnp.where(c == m, iota, float(rows)), axis=0, keepdims=True)
        val_ref[pl.ds(r, 1), :] = m
        idx_ref[pl.ds(r, 1), :] = idx
        c = jnp.where(iota == idx, -jnp.inf, c)


def _peer_route_body(h_ref, g_ref, wq_ref, keys_ref, sela_ref, selb_ref, pad_ref,
                     xn_ref, rows_ref, shift_ref, gate_ref,
                     q_ref, v_ref, i_ref, best_ref, pos_ref, e_ref, p_ref):
    hx = h_ref[...]
    xn = hx * lax.rsqrt(jnp.mean(hx * hx, axis=-1, keepdims=True) + EPS) * g_ref[...]
    xn_ref[...] = xn
    xb = xn.astype(jnp.bfloat16)
    for hh in range(PEER_HEADS):
        qh = jnp.dot(xb, wq_ref[:, hh * PEER_QDIM:(hh + 1) * PEER_QDIM],
                     preferred_element_type=jnp.float32).astype(jnp.bfloat16)
        q_ref[2 * hh] = qh[:, :PEER_HALF]
        q_ref[2 * hh + 1] = qh[:, PEER_HALF:]

    for lt in range(ROUTE_TOKEN_TILE // LANES):
        tok = pl.ds(lt * LANES, LANES)

        def head(hh, carry):
            for half in range(2):
                j = 2 * hh + half
                s_t = lax.dot_general(keys_ref[j], q_ref[j, tok, :], (((1,), (1,)), ((), ())),
                                      preferred_element_type=jnp.float32)
                _top16_rows(s_t, v_ref.at[half], i_ref.at[half])
            v1, v2 = v_ref[0], v_ref[1]
            i1, i2 = i_ref[0], i_ref[1]
            hi = lax.Precision.HIGHEST
            cand = (jnp.dot(sela_ref[...], v1, precision=hi, preferred_element_type=jnp.float32)
                    + jnp.dot(selb_ref[...], v2, precision=hi, preferred_element_type=jnp.float32)
                    + pad_ref[...])
            ea = jnp.dot(sela_ref[...], i1, preferred_element_type=jnp.float32)
            eb = jnp.dot(selb_ref[...], i2, preferred_element_type=jnp.float32)
            cand_e = (ea * PEER_KEYS + eb).astype(jnp.int32)
            _top16_rows(cand, best_ref, pos_ref)
            best = best_ref[...]
            pos = pos_ref[...]
            crow = lax.broadcasted_iota(jnp.int32, cand.shape, 0).astype(jnp.float32)
            picks = []
            for r in range(PEER_TOPK):
                hit = crow == pos[r:r + 1, :]
                picks.append(jnp.sum(jnp.where(hit, cand_e, 0), axis=0, keepdims=True))
            w = jnp.exp(best - best[0:1, :])
            gates = w / jnp.sum(w, axis=0, keepdims=True)
            base = pl.multiple_of(hh * PEER_TOPK, PEER_TOPK)
            p_ref[pl.ds(base, PEER_TOPK), :] = gates
            e_ref[pl.ds(base, PEER_TOPK), :] = jnp.concatenate(picks, axis=0)
            return carry

        lax.fori_loop(0, PEER_HEADS, head, 0)
        e_t = e_ref[...].T
        rows_ref[tok, :] = (e_t & (PACKED_ROWS - 1)) * VREG_SUBLANES
        shift_ref[tok, :] = (1 - (e_t >> 13)) * 16
        gate_ref[tok, :] = p_ref[...].T


def _peer_route(h, g_ffn, w_q, sub_keys):
    T, d = h.shape
    tt = ROUTE_TOKEN_TILE
    assert T % tt == 0 and PACKED_ROWS == 1 << 13
    sel_a, sel_b, pad = _stair_constants()
    keys = sub_keys.reshape(PEER_HEADS * 2, PEER_KEYS, PEER_HALF).astype(jnp.bfloat16)
    tok_spec = pl.BlockSpec((tt, d), lambda i: (i, 0))
    pick_spec = pl.BlockSpec((tt, PEER_PICKS), lambda i: (i, 0))

    def whole(shape):
        return pl.BlockSpec(shape, lambda i: (0,) * len(shape))

    return pl.pallas_call(
        _peer_route_body,
        grid=(T // tt,),
        in_specs=[tok_spec, whole((1, d)), whole((d, PEER_HEADS * PEER_QDIM)),
                  whole(keys.shape), whole(sel_a.shape), whole(sel_b.shape), whole(pad.shape)],
        out_specs=[tok_spec, pick_spec, pick_spec, pick_spec],
        out_shape=[jax.ShapeDtypeStruct((T, d), jnp.float32),
                   jax.ShapeDtypeStruct((T, PEER_PICKS), jnp.int32),
                   jax.ShapeDtypeStruct((T, PEER_PICKS), jnp.int32),
                   jax.ShapeDtypeStruct((T, PEER_PICKS), jnp.float32)],
        scratch_shapes=[pltpu.VMEM((2 * PEER_HEADS, tt, PEER_HALF), jnp.bfloat16),
                        pltpu.VMEM((2, PEER_TOPK, LANES), jnp.float32),
                        pltpu.VMEM((2, PEER_TOPK, LANES), jnp.float32),
                        pltpu.VMEM((PEER_TOPK, LANES), jnp.float32),
                        pltpu.VMEM((PEER_TOPK, LANES), jnp.float32),
                        pltpu.VMEM((PEER_PICKS, LANES), jnp.int32),
                        pltpu.VMEM((PEER_PICKS, LANES), jnp.float32)],
        compiler_params=pltpu.CompilerParams(dimension_semantics=("arbitrary",)),
        name="peer_route",
    )(h, g_ffn.reshape(1, d), w_q.astype(jnp.bfloat16), keys, sel_a, sel_b, pad)


ROW_SUBLANES = D_MODEL // LANES
PEER_TOKEN_TILE = 64
PACKED_TABLE_BYTES = PACKED_ROWS * D_MODEL * 4
BLOCK_TOKENS = 8
LANE_REPLICATED_BYTES = BLOCK_TOKENS * PEER_PICKS * LANES * 4
PEER_VMEM_LIMIT = PACKED_TABLE_BYTES + 4 * LANE_REPLICATED_BYTES + 8 * 1024 * 1024
HIGH_HALF_MASK = 0xFFFF0000
_BITREV8 = (0, 4, 2, 6, 1, 5, 3, 7)


def _pack_expert_table(tab):
    bits = lax.bitcast_convert_type(tab.astype(jnp.bfloat16), jnp.uint16).astype(jnp.uint32)
    packed = bits[:PACKED_ROWS] | (bits[PACKED_ROWS:] << 16)
    return packed.reshape(PACKED_ROWS * ROW_SUBLANES, LANES)


def _gelu_exact(s):
    return 0.5 * s * (1.0 + lax.erf(s * (2.0 ** -0.5)))


OUT_PARTIAL_SUMS = 4


def _replicate(src_ref, t, dst_ref, slot):
    dst_ref[slot] = jnp.broadcast_to(src_ref[pl.ds(t, 1), :], (PEER_PICKS, LANES)).T


def _expert_row(tab_ref, row8, shift_ref, slot, k):
    w = tab_ref[pl.ds(pl.multiple_of(row8, ROW_SUBLANES), ROW_SUBLANES), :]
    s = jnp.broadcast_to(shift_ref[slot, pl.ds(k, 1), :],
                         (ROW_SUBLANES, LANES)).astype(jnp.uint32)
    return lax.bitcast_convert_type((w << s) & jnp.uint32(HIGH_HALF_MASK), jnp.float32)


def _fold_pair(a, b, shift, keep_a):
    if 2 * shift == ROW_SUBLANES:
        return jnp.where(keep_a, a, b) + pltpu.roll(jnp.where(keep_a, b, a), shift, axis=0)
    return (jnp.where(keep_a, a, pltpu.roll(b, shift, axis=0))
            + jnp.where(keep_a, pltpu.roll(a, ROW_SUBLANES - shift, axis=0), b))


def _sublane_sums(tiles, masks):
    level = [tiles[_BITREV8[s]] for s in range(ROW_SUBLANES)]
    for shift, keep in masks:
        level = [_fold_pair(level[2 * i], level[2 * i + 1], shift, keep)
                 for i in range(len(level) // 2)]
    return level[0]


def _two_buffer_token_loop(step):
    def block_pair(i, carry):
        base = i * (2 * BLOCK_TOKENS)
        for cur in range(2):
            def one(u, c, cur=cur):
                step(base + cur * BLOCK_TOKENS + u, u, cur, 1 - cur)
                return c

            lax.fori_loop(0, BLOCK_TOKENS, one, 0)
        return carry

    lax.fori_loop(0, PEER_TOKEN_TILE // (2 * BLOCK_TOKENS), block_pair, 0)


def _peer_hidden_body(row_ref, shift_ref, x_ref, gate_ref, utab_ref, act_ref,
                      sh0_ref, sh1_ref, part0_ref, part1_ref, s_ref):
    sub = lax.broadcasted_iota(jnp.int32, (ROW_SUBLANES, LANES), 0)
    masks = [(4, (sub & 4) == 0), (2, (sub & 2) == 0), (1, (sub & 1) == 0)]
    sh = (sh0_ref, sh1_ref)
    part = (part0_ref, part1_ref)
    last = PEER_TOKEN_TILE - 1

    def lane_sums(src_ref, slot, t):
        s_ref[pl.ds(t, 1), :] = jnp.sum(src_ref[slot].T, axis=0, keepdims=True)

    for u in range(BLOCK_TOKENS):
        _replicate(shift_ref, u, sh0_ref, u)
    part1_ref[...] = jnp.zeros(part1_ref.shape, jnp.float32)

    def step(t, slot, cur, nxt):
        x = x_ref[t]
        for g in range(PEER_PICKS // ROW_SUBLANES):
            prods = [_expert_row(utab_ref, row_ref[t, g * ROW_SUBLANES + j], sh[cur], slot,
                                 g * ROW_SUBLANES + j) * x
                     for j in range(ROW_SUBLANES)]
            part[cur][slot, pl.ds(g * ROW_SUBLANES, ROW_SUBLANES), :] = _sublane_sums(prods, masks)
        _replicate(shift_ref, jnp.minimum(t + BLOCK_TOKENS, last), sh[nxt], slot)
        lane_sums(part[nxt], slot, jnp.maximum(t - BLOCK_TOKENS, 0))

    _two_buffer_token_loop(step)
    for u in range(BLOCK_TOKENS):
        lane_sums(part1_ref, u, PEER_TOKEN_TILE - BLOCK_TOKENS + u)
    act_ref[...] = _gelu_exact(s_ref[...]) * gate_ref[...]


def _peer_out_body(row_ref, shift_ref, act_ref, vtab_ref, y_ref,
                   abc0_ref, abc1_ref, sh0_ref, sh1_ref):
    abc = (abc0_ref, abc1_ref)
    sh = (sh0_ref, sh1_ref)
    last = PEER_TOKEN_TILE - 1
    for u in range(BLOCK_TOKENS):
        _replicate(act_ref, u, abc0_ref, u)
        _replicate(shift_ref, u, sh0_ref, u)

    def step(t, slot, cur, nxt):
        accs = [jnp.zeros((ROW_SUBLANES, LANES), jnp.float32) for _ in range(OUT_PARTIAL_SUMS)]
        for k in range(PEER_PICKS):
            r = _expert_row(vtab_ref, row_ref[t, k], sh[cur], slot, k)
            a = jnp.broadcast_to(abc[cur][slot, pl.ds(k, 1), :], (ROW_SUBLANES, LANES))
            accs[k % OUT_PARTIAL_SUMS] = accs[k % OUT_PARTIAL_SUMS] + a * r
        y_ref[t] = (accs[0] + accs[1]) + (accs[2] + accs[3])
        ahead = jnp.minimum(t + BLOCK_TOKENS, last)
        _replicate(act_ref, ahead, abc[nxt], slot)
        _replicate(shift_ref, ahead, sh[nxt], slot)

    _two_buffer_token_loop(step)


def _peer_experts(xn, rows8, shifts, gates, expert_u, expert_v):
    T, d = xn.shape
    tt = PEER_TOKEN_TILE
    assert T % tt == 0 and d == ROW_SUBLANES * LANES and ROW_SUBLANES == VREG_SUBLANES
    assert tt % (2 * BLOCK_TOKENS) == 0
    x3 = xn.reshape(T, ROW_SUBLANES, LANES)
    smem_spec = pl.BlockSpec((tt, PEER_PICKS), lambda i: (i, 0), memory_space=pltpu.SMEM)
    pick_spec = pl.BlockSpec((tt, PEER_PICKS), lambda i: (i, 0))
    tok_spec = pl.BlockSpec((tt, ROW_SUBLANES, LANES), lambda i: (i, 0, 0))
    table_spec = pl.BlockSpec(memory_space=pltpu.VMEM)
    params = pltpu.CompilerParams(dimension_semantics=("arbitrary",),
                                  vmem_limit_bytes=PEER_VMEM_LIMIT)

    def replicated(dtype):
        return pltpu.VMEM((BLOCK_TOKENS, PEER_PICKS, LANES), dtype)

    act = pl.pallas_call(
        _peer_hidden_body,
        grid=(T // tt,),
        in_specs=[smem_spec, pick_spec, tok_spec, pick_spec, table_spec],
        out_specs=pick_spec,
        out_shape=jax.ShapeDtypeStruct((T, PEER_PICKS), jnp.float32),
        scratch_shapes=[replicated(jnp.int32), replicated(jnp.int32),
                        replicated(jnp.float32), replicated(jnp.float32),
                        pltpu.VMEM((tt, PEER_PICKS), jnp.float32)],
        compiler_params=params,
        name="peer_hidden",
    )(rows8, shifts, x3, gates, _pack_expert_table(expert_u))
    y = pl.pallas_call(
        _peer_out_body,
        grid=(T // tt,),
        in_specs=[smem_spec, pick_spec, pick_spec, table_spec],
        out_specs=tok_spec,
        out_shape=jax.ShapeDtypeStruct((T, ROW_SUBLANES, LANES), jnp.float32),
        scratch_shapes=[replicated(jnp.float32), replicated(jnp.float32),
                        replicated(jnp.int32), replicated(jnp.int32)],
        compiler_params=params,
        name="peer_out",
    )(rows8, shifts, act, _pack_expert_table(expert_v))
    return y.reshape(T, d)


def _peer_ffn(h, g_ffn, w_q, sub_keys, expert_u, expert_v):
    xn, rows8, shifts, gates = _peer_route(h, g_ffn, w_q, sub_keys)
    return _peer_experts(xn, rows8, shifts, gates, expert_u, expert_v)


def _final_norm_body(h_ref, f_ref, g_ref, o_ref):
    xf = h_ref[...] + f_ref[...]
    ms = jnp.mean(xf * xf, axis=-1, keepdims=True)
    o_ref[...] = xf * lax.rsqrt(ms + EPS) * g_ref[...]


def _final_norm(h, ffn, g):
    rows, d = h.shape
    tile = 512
    return pl.pallas_call(
        _final_norm_body,
        grid=(rows // tile,),
        in_specs=[pl.BlockSpec((tile, d), lambda i: (i, 0)),
                  pl.BlockSpec((tile, d), lambda i: (i, 0)),
                  pl.BlockSpec((1, d), lambda i: (0, 0))],
        out_specs=pl.BlockSpec((tile, d), lambda i: (i, 0)),
        out_shape=jax.ShapeDtypeStruct((rows, d), jnp.float32),
        name="final_norm",
    )(h, ffn, g.reshape(1, d))


def kernel(x, meta_tokens, g_mix, w_in, conv_ssd_w, conv_ssd_b, dt_bias, a_log, d_skip,
           ssd_norm_w, conv_sc_w, w_out, g_ffn, w_q, sub_keys, expert_u, expert_v, g_final):
    bsz, seq, d = x.shape
    assert g_mix.shape[0] == 1
    h = _mixer(x, meta_tokens, g_mix[0], w_in[0], conv_ssd_w[0], conv_ssd_b[0], dt_bias[0],
               a_log[0], d_skip[0], ssd_norm_w[0], conv_sc_w[0], w_out[0])
    ffn = _peer_ffn(h, g_ffn[0], w_q[0], sub_keys[0], expert_u[0], expert_v[0])
    return _final_norm(h, ffn, g_final).reshape(bsz, seq, d)
```

```python
import numpy as np

import jax
import jax.numpy as jnp
from jax import lax
from jax.experimental import pallas as pl
from jax.experimental.pallas import tpu as pltpu

D_MODEL = 1024
N_META = 16
D_MIX = 2 * D_MODEL
SSD_HEAD_DIM = 64
SSD_HEADS = 24
D_SSD = SSD_HEADS * SSD_HEAD_DIM
SSD_GROUPS = 4
SSD_STATE = 128
SSD_CONV = 4
SSD_CHUNK = 128
D_XBC = D_SSD + 2 * SSD_GROUPS * SSD_STATE
D_SC = D_MIX - D_SSD
SC_CONV = 3
D_IN = D_SSD + D_XBC + SSD_HEADS + 3 * D_SC
PEER_HEADS = 8
PEER_KEYS = 128
PEER_TOPK = 16
PEER_QDIM = 256
PEER_HALF = PEER_QDIM // 2
EPS = 1e-6

LANES = 128
VREG_SUBLANES = 8


SSD_PAD = SSD_CHUNK - N_META
HEADS_PER_GROUP = SSD_HEADS // SSD_GROUPS
D_GROUP = HEADS_PER_GROUP * SSD_HEAD_DIM
D_BC = SSD_GROUPS * SSD_STATE
D_SC3 = 3 * D_SC
D_IN_ALIGNED = D_SSD + D_XBC + LANES + D_SC3
INPROJ_TOKEN_TILE = 256
INPROJ_COL_TILE = 512
HALO_ROWS = VREG_SUBLANES


def _inproj_body(h_ref, g_ref, w_ref, z_ref, xbc_ref, dt_ref, sc_ref):
    hx = h_ref[...]
    xn = hx * lax.rsqrt(jnp.mean(hx * hx, axis=-1, keepdims=True) + EPS) * g_ref[...]
    xb = xn.astype(jnp.bfloat16)
    col = 0
    for out_ref in (z_ref, xbc_ref, dt_ref, sc_ref):
        width = out_ref.shape[1]
        for c0 in range(0, width, INPROJ_COL_TILE):
            cw = min(INPROJ_COL_TILE, width - c0)
            out_ref[:, c0:c0 + cw] = jnp.dot(xb, w_ref[:, col + c0:col + c0 + cw],
                                             preferred_element_type=jnp.float32)
        col += width


def _inproj(h_pad, g_mix, w_in):
    rows, d = h_pad.shape
    tm = INPROJ_TOKEN_TILE
    assert rows % tm == 0
    s1, s2, s3 = D_SSD, D_SSD + D_XBC, D_SSD + D_XBC + SSD_HEADS
    w = jnp.concatenate([w_in[:, :s2], w_in[:, s2:s3],
                         jnp.zeros((d, LANES - SSD_HEADS), w_in.dtype), w_in[:, s3:]],
                        axis=1).astype(jnp.bfloat16)
    widths = (D_SSD, D_XBC, LANES, D_SC3)
    return pl.pallas_call(
        _inproj_body,
        grid=(rows // tm,),
        in_specs=[pl.BlockSpec((tm, d), lambda i: (i, 0)),
                  pl.BlockSpec((1, d), lambda i: (0, 0)),
                  pl.BlockSpec((d, D_IN_ALIGNED), lambda i: (0, 0))],
        out_specs=[pl.BlockSpec((tm, wd), lambda i: (i, 0)) for wd in widths],
        out_shape=[jax.ShapeDtypeStruct((rows, wd), jnp.float32) for wd in widths],
        compiler_params=pltpu.CompilerParams(dimension_semantics=("arbitrary",)),
        name="mixer_inproj",
    )(h_pad, g_mix.reshape(1, d), w)


def _silu(v):
    return v / (1.0 + jnp.exp(-v))


def _mixer_chunk_body(z_ref, xbc_ref, dt_ref, sc_ref, x_ref, cw_ref, cb_ref, dtb_ref, alog_ref,
                      dskip_ref, nw_ref, scw_ref, expand_ref, wout_ref, o_ref,
                      ext_ref, ext2_ref, state_ref):
    c = pl.program_id(1)
    q = SSD_CHUNK
    hi = lax.Precision.HIGHEST

    @pl.when(c == 0)
    def _():
        ext_ref[0:HALO_ROWS, :] = jnp.zeros((HALO_ROWS, D_XBC), jnp.float32)
        ext2_ref[0:HALO_ROWS, :] = jnp.zeros((HALO_ROWS, D_SC), jnp.float32)
        state_ref[...] = jnp.zeros(state_ref.shape, jnp.float32)

    row = lax.broadcasted_iota(jnp.int32, (q, 1), 0)
    live = jnp.logical_or(c > 0, row >= SSD_PAD)

    ext_ref[HALO_ROWS:HALO_ROWS + q, :] = xbc_ref[...]
    conv = jnp.broadcast_to(cb_ref[...], (q, D_XBC))
    for j in range(SSD_CONV):
        conv = conv + cw_ref[j:j + 1, :] * ext_ref[pl.ds(HALO_ROWS - (SSD_CONV - 1) + j, q), :]
    ext_ref[0:HALO_ROWS, :] = ext_ref[q:q + HALO_ROWS, :]
    xbc = jnp.where(live, _silu(conv), 0.0)
    xs = xbc[:, :D_SSD]
    bm = xbc[:, D_SSD:D_SSD + D_BC]
    cm = xbc[:, D_SSD + D_BC:]

    dtr = dt_ref[...] + dtb_ref[...]
    dt = jnp.maximum(dtr, 0.0) + jnp.log1p(jnp.exp(-jnp.abs(dtr)))
    dt = jnp.where(live, dt, 0.0)
    a = dt * (-jnp.exp(alog_ref[...]))
    tri = (lax.broadcasted_iota(jnp.int32, (q, q), 0)
           >= lax.broadcasted_iota(jnp.int32, (q, q), 1))
    a_cs = jnp.dot(tri.astype(jnp.float32), a, precision=hi,
                   preferred_element_type=jnp.float32)
    a_cs_t = a_cs.T
    a_last = a_cs[q - 1:q, :]
    expand = expand_ref[...]
    dt_w = jnp.dot(dt, expand, precision=hi, preferred_element_type=jnp.float32)
    acs_w = jnp.dot(a_cs, expand, precision=hi, preferred_element_type=jnp.float32)
    alast_w = jnp.dot(a_last, expand, precision=hi, preferred_element_type=jnp.float32)
    x_dt = xs * dt_w
    grow = jnp.exp(acs_w)
    x_tail = (x_dt * jnp.exp(alast_w - acs_w)).astype(jnp.bfloat16)
    x_dt_b = x_dt.astype(jnp.bfloat16)
    lane = lax.broadcasted_iota(jnp.int32, (q, LANES), 1)

    y_groups = []
    for g in range(SSD_GROUPS):
        gs = slice(g * D_GROUP, (g + 1) * D_GROUP)
        bg = bm[:, g * SSD_STATE:(g + 1) * SSD_STATE]
        cg = cm[:, g * SSD_STATE:(g + 1) * SSD_STATE].astype(jnp.bfloat16)
        cb = lax.dot_general(cg, bg.astype(jnp.bfloat16), (((1,), (1,)), ((), ())),
                             preferred_element_type=jnp.float32)
        state = state_ref[g]
        y_off = jnp.dot(cg, state.astype(jnp.bfloat16),
                        preferred_element_type=jnp.float32) * grow[:, gs]
        y_pairs = []
        for pair in range(HEADS_PER_GROUP // 2):
            cols = slice(g * D_GROUP + pair * LANES, g * D_GROUP + (pair + 1) * LANES)
            xp = x_dt_b[:, cols]
            halves = []
            for sub in range(2):
                hd = g * HEADS_PER_GROUP + 2 * pair + sub
                seg = (jnp.broadcast_to(a_cs[:, hd:hd + 1], (q, q))
                       - jnp.broadcast_to(a_cs_t[hd:hd + 1, :], (q, q)))
                decay = jnp.exp(jnp.where(tri, seg, -jnp.inf))
                halves.append(jnp.dot((cb * decay).astype(jnp.bfloat16), xp,
                                      preferred_element_type=jnp.float32))
            y_pairs.append(jnp.where(lane < SSD_HEAD_DIM, halves[0], halves[1]))
        y_groups.append(jnp.concatenate(y_pairs, axis=1) + y_off)
        bg_t = bg.T.astype(jnp.bfloat16)
        state_ref[g] = (state * jnp.exp(alast_w[:, gs])
                        + jnp.dot(bg_t, x_tail[:, gs], preferred_element_type=jnp.float32))

    y = jnp.concatenate(y_groups, axis=1) + dskip_ref[...] * xs
    y = y * _silu(z_ref[...])
    normed = []
    for g in range(SSD_GROUPS):
        yg = y[:, g * D_GROUP:(g + 1) * D_GROUP]
        ms = jnp.mean(yg * yg, axis=-1, keepdims=True)
        normed.append(yg * lax.rsqrt(ms + EPS))
    y_ssd = jnp.concatenate(normed, axis=1) * nw_ref[...]

    sc = sc_ref[...]
    ext2_ref[HALO_ROWS:HALO_ROWS + q, :] = sc[:, D_SC:2 * D_SC] * sc[:, 2 * D_SC:]
    conv2 = jnp.zeros((q, D_SC), jnp.float32)
    for j in range(SC_CONV):
        conv2 = conv2 + scw_ref[j:j + 1, :] * ext2_ref[pl.ds(HALO_ROWS - (SC_CONV - 1) + j, q), :]
    ext2_ref[0:HALO_ROWS, :] = ext2_ref[q:q + HALO_ROWS, :]
    y_sc = sc[:, :D_SC] * conv2

    @pl.when(c > 0)
    def _():
        ycat = jnp.concatenate([y_ssd, y_sc], axis=1).astype(jnp.bfloat16)
        o_ref[...] = x_ref[...] + jnp.dot(ycat, wout_ref[...],
                                          preferred_element_type=jnp.float32)


def _head_expand_matrix():
    e = np.zeros((LANES, D_SSD), np.float32)
    for hd in range(SSD_HEADS):
        e[hd, hd * SSD_HEAD_DIM:(hd + 1) * SSD_HEAD_DIM] = 1.0
    return jnp.asarray(e)


def _mixer(x, meta_tokens, g_mix, w_in, conv_ssd_w, conv_ssd_b, dt_bias, a_log, d_skip,
           ssd_norm_w, conv_sc_w, w_out):
    bsz, seq, d = x.shape
    assert seq % SSD_CHUNK == 0
    n_chunks = seq // SSD_CHUNK + 1
    meta = jnp.broadcast_to(meta_tokens[None].astype(x.dtype), (bsz, N_META, d))
    h_pad = jnp.concatenate([jnp.zeros((bsz, SSD_PAD, d), x.dtype), meta, x], axis=1)
    z, xbc, dt_raw, sc = _inproj(h_pad.reshape(bsz * n_chunks * SSD_CHUNK, d), g_mix, w_in)

    def pad_heads(v):
        return jnp.pad(v.astype(jnp.float32), (0, LANES - SSD_HEADS)).reshape(1, LANES)

    params = [conv_ssd_w, conv_ssd_b.reshape(1, D_XBC), pad_heads(dt_bias), pad_heads(a_log),
              jnp.repeat(d_skip.astype(jnp.float32), SSD_HEAD_DIM).reshape(1, D_SSD),
              ssd_norm_w.reshape(1, D_SSD), conv_sc_w, _head_expand_matrix(),
              w_out.astype(jnp.bfloat16)]

    def chunk_rows(width):
        return pl.BlockSpec((SSD_CHUNK, width), lambda b, c: (b * n_chunks + c, 0))

    def seq_rows(b, c):
        return (b * (n_chunks - 1) + jnp.maximum(c - 1, 0), 0)

    def whole(arr):
        return pl.BlockSpec(arr.shape, lambda b, c: (0,) * arr.ndim)

    return pl.pallas_call(
        _mixer_chunk_body,
        grid=(bsz, n_chunks),
        in_specs=[chunk_rows(D_SSD), chunk_rows(D_XBC), chunk_rows(LANES), chunk_rows(D_SC3),
                  pl.BlockSpec((SSD_CHUNK, d), seq_rows)] + [whole(p) for p in params],
        out_specs=pl.BlockSpec((SSD_CHUNK, d), seq_rows),
        out_shape=jax.ShapeDtypeStruct((bsz * seq, d), jnp.float32),
        scratch_shapes=[pltpu.VMEM((HALO_ROWS + SSD_CHUNK, D_XBC), jnp.float32),
                        pltpu.VMEM((HALO_ROWS + SSD_CHUNK, D_SC), jnp.float32),
                        pltpu.VMEM((SSD_GROUPS, SSD_STATE, D_GROUP), jnp.float32)],
        compiler_params=pltpu.CompilerParams(dimension_semantics=("arbitrary", "arbitrary")),
        name="mixer_chunk",
    )(z, xbc, dt_raw, sc, x.reshape(bsz * seq, d), *params)


PEER_PICKS = PEER_HEADS * PEER_TOPK
PACKED_ROWS = PEER_KEYS * PEER_KEYS // 2
ROUTE_TOKEN_TILE = 256

_STAIR = [(a, b) for a in range(PEER_TOPK) for b in range(PEER_TOPK)
          if (a + 1) * (b + 1) <= PEER_TOPK]
STAIR_ROWS = -(-len(_STAIR) // VREG_SUBLANES) * VREG_SUBLANES


def _stair_constants():
    sel_a = np.zeros((STAIR_ROWS, PEER_TOPK), np.float32)
    sel_b = np.zeros((STAIR_ROWS, PEER_TOPK), np.float32)
    pad = np.full((STAIR_ROWS, LANES), -np.inf, np.float32)
    for c, (a, b) in enumerate(_STAIR):
        sel_a[c, a] = 1.0
        sel_b[c, b] = 1.0
        pad[c, :] = 0.0
    return jnp.asarray(sel_a), jnp.asarray(sel_b), jnp.asarray(pad)


def _top16_rows(c, val_ref, idx_ref):
    rows = c.shape[0]
    iota = lax.broadcasted_iota(jnp.int32, c.shape, 0).astype(jnp.float32)
    for r in range(PEER_TOPK):
        m = jnp.max(c, axis=0, keepdims=True)
        idx = jnp.min(jnp.where(c == m, iota, float(rows)), axis=0, keepdims=True)
        val_ref[pl.ds(r, 1), :] = m
        idx_ref[pl.ds(r, 1), :] = idx
        c = jnp.where(iota == idx, -jnp.inf, c)


def _peer_route_body(h_ref, g_ref, wq_ref, keys_ref, sela_ref, selb_ref, pad_ref,
                     xn_ref, rows_ref, shift_ref, gate_ref,
                     q_ref, v_ref, i_ref, best_ref, pos_ref, e_ref, p_ref):
    hx = h_ref[...]
    xn = hx * lax.rsqrt(jnp.mean(hx * hx, axis=-1, keepdims=True) + EPS) * g_ref[...]
    xn_ref[...] = xn
    xb = xn.astype(jnp.bfloat16)
    for hh in range(PEER_HEADS):
        qh = jnp.dot(xb, wq_ref[:, hh * PEER_QDIM:(hh + 1) * PEER_QDIM],
                     preferred_element_type=jnp.float32).astype(jnp.bfloat16)
        q_ref[2 * hh] = qh[:, :PEER_HALF]
        q_ref[2 * hh + 1] = qh[:, PEER_HALF:]

    for lt in range(ROUTE_TOKEN_TILE // LANES):
        tok = pl.ds(lt * LANES, LANES)

        def head(hh, carry):
            for half in range(2):
                j = 2 * hh + half
                s_t = lax.dot_general(keys_ref[j], q_ref[j, tok, :], (((1,), (1,)), ((), ())),
                                      preferred_element_type=jnp.float32)
                _top16_rows(s_t, v_ref.at[half], i_ref.at[half])
            v1, v2 = v_ref[0], v_ref[1]
            i1, i2 = i_ref[0], i_ref[1]
            hi = lax.Precision.HIGHEST
            cand = (jnp.dot(sela_ref[...], v1, precision=hi, preferred_element_type=jnp.float32)
                    + jnp.dot(selb_ref[...], v2, precision=hi, preferred_element_type=jnp.float32)
                    + pad_ref[...])
            ea = jnp.dot(sela_ref[...], i1, preferred_element_type=jnp.float32)
            eb = jnp.dot(selb_ref[...], i2, preferred_element_type=jnp.float32)
            cand_e = (ea * PEER_KEYS + eb).astype(jnp.int32)
            _top16_rows(cand, best_ref, pos_ref)
            best = best_ref[...]
            pos = pos_ref[...]
            crow = lax.broadcasted_iota(jnp.int32, cand.shape, 0).astype(jnp.float32)
            picks = []
            for r in range(PEER_TOPK):
                hit = crow == pos[r:r + 1, :]
                picks.append(jnp.sum(jnp.where(hit, cand_e, 0), axis=0, keepdims=True))
            w = jnp.exp(best - best[0:1, :])
            gates = w / jnp.sum(w, axis=0, keepdims=True)
            base = pl.multiple_of(hh * PEER_TOPK, PEER_TOPK)
            p_ref[pl.ds(base, PEER_TOPK), :] = gates
            e_ref[pl.ds(base, PEER_TOPK), :] = jnp.concatenate(picks, axis=0)
            return carry

        lax.fori_loop(0, PEER_HEADS, head, 0)
        e_t = e_ref[...].T
        rows_ref[tok, :] = (e_t & (PACKED_ROWS - 1)) * VREG_SUBLANES
        shift_ref[tok, :] = (1 - (e_t >> 13)) * 16
        gate_ref[tok, :] = p_ref[...].T


def _peer_route(h, g_ffn, w_q, sub_keys):
    T, d = h.shape
    tt = ROUTE_TOKEN_TILE
    assert T % tt == 0 and PACKED_ROWS == 1 << 13
    sel_a, sel_b, pad = _stair_constants()
    keys = sub_keys.reshape(PEER_HEADS * 2, PEER_KEYS, PEER_HALF).astype(jnp.bfloat16)
    tok_spec = pl.BlockSpec((tt, d), lambda i: (i, 0))
    pick_spec = pl.BlockSpec((tt, PEER_PICKS), lambda i: (i, 0))

    def whole(shape):
        return pl.BlockSpec(shape, lambda i: (0,) * len(shape))

    return pl.pallas_call(
        _peer_route_body,
        grid=(T // tt,),
        in_specs=[tok_spec, whole((1, d)), whole((d, PEER_HEADS * PEER_QDIM)),
                  whole(keys.shape), whole(sel_a.shape), whole(sel_b.shape), whole(pad.shape)],
        out_specs=[tok_spec, pick_spec, pick_spec, pick_spec],
        out_shape=[jax.ShapeDtypeStruct((T, d), jnp.float32),
                   jax.ShapeDtypeStruct((T, PEER_PICKS), jnp.int32),
                   jax.ShapeDtypeStruct((T, PEER_PICKS), jnp.int32),
                   jax.ShapeDtypeStruct((T, PEER_PICKS), jnp.float32)],
        scratch_shapes=[pltpu.VMEM((2 * PEER_HEADS, tt, PEER_HALF), jnp.bfloat16),
                        pltpu.VMEM((2, PEER_TOPK, LANES), jnp.float32),
                        pltpu.VMEM((2, PEER_TOPK, LANES), jnp.float32),
                        pltpu.VMEM((PEER_TOPK, LANES), jnp.float32),
                        pltpu.VMEM((PEER_TOPK, LANES), jnp.float32),
                        pltpu.VMEM((PEER_PICKS, LANES), jnp.int32),
                        pltpu.VMEM((PEER_PICKS, LANES), jnp.float32)],
        compiler_params=pltpu.CompilerParams(dimension_semantics=("arbitrary",)),
        name="peer_route",
    )(h, g_ffn.reshape(1, d), w_q.astype(jnp.bfloat16), keys, sel_a, sel_b, pad)


ROW_SUBLANES = D_MODEL // LANES
PEER_TOKEN_TILE = 64
PACKED_TABLE_BYTES = PACKED_ROWS * D_MODEL * 4
BLOCK_TOKENS = 8
LANE_REPLICATED_BYTES = BLOCK_TOKENS * PEER_PICKS * LANES * 4
PEER_VMEM_LIMIT = PACKED_TABLE_BYTES + 4 * LANE_REPLICATED_BYTES + 8 * 1024 * 1024
HIGH_HALF_MASK = 0xFFFF0000
SHIFT_FIELD_MASK = 0x1F
_BITREV8 = (0, 4, 2, 6, 1, 5, 3, 7)


def _pack_expert_table(tab):
    bits = lax.bitcast_convert_type(tab.astype(jnp.bfloat16), jnp.uint16).astype(jnp.uint32)
    packed = bits[:PACKED_ROWS] | (bits[PACKED_ROWS:] << 16)
    return packed.reshape(PACKED_ROWS * ROW_SUBLANES, LANES)


def _gelu_exact(s):
    return 0.5 * s * (1.0 + lax.erf(s * (2.0 ** -0.5)))


OUT_PARTIAL_SUMS = 4


def _replicate(src_ref, t, dst_ref, slot):
    dst_ref[slot] = jnp.broadcast_to(src_ref[pl.ds(t, 1), :], (PEER_PICKS, LANES)).T


def _splat_row(ref, slot, k):
    return jnp.broadcast_to(ref[slot, pl.ds(k, 1), :], (ROW_SUBLANES, LANES))


def _expert_row(tab_ref, row8, shift):
    w = tab_ref[pl.ds(pl.multiple_of(row8, ROW_SUBLANES), ROW_SUBLANES), :]
    return lax.bitcast_convert_type((w << shift) & jnp.uint32(HIGH_HALF_MASK), jnp.float32)


def _pack_act(act, shift):
    bits = lax.bitcast_convert_type(act, jnp.uint32)
    rounded = bits + jnp.uint32(0x7FFF) + ((bits >> 16) & jnp.uint32(1))
    return (rounded & jnp.uint32(HIGH_HALF_MASK)) | shift.astype(jnp.uint32)


def _fold_pair(a, b, shift, keep_a):
    if 2 * shift == ROW_SUBLANES:
        return jnp.where(keep_a, a, b) + pltpu.roll(jnp.where(keep_a, b, a), shift, axis=0)
    return (jnp.where(keep_a, a, pltpu.roll(b, shift, axis=0))
            + jnp.where(keep_a, pltpu.roll(a, ROW_SUBLANES - shift, axis=0), b))


def _sublane_sums(tiles, masks):
    level = [tiles[_BITREV8[s]] for s in range(ROW_SUBLANES)]
    for shift, keep in masks:
        level = [_fold_pair(level[2 * i], level[2 * i + 1], shift, keep)
                 for i in range(len(level) // 2)]
    return level[0]


def _two_buffer_token_loop(step):
    def block_pair(i, carry):
        base = i * (2 * BLOCK_TOKENS)
        for cur in range(2):
            def one(u, c, cur=cur):
                step(base + cur * BLOCK_TOKENS + u, u, cur, 1 - cur)
                return c

            lax.fori_loop(0, BLOCK_TOKENS, one, 0)
        return carry

    lax.fori_loop(0, PEER_TOKEN_TILE // (2 * BLOCK_TOKENS), block_pair, 0)


def _peer_hidden_body(row_ref, shift_ref, x_ref, gate_ref, utab_ref, act_ref,
                      sh0_ref, sh1_ref, part0_ref, part1_ref, s_ref):
    sub = lax.broadcasted_iota(jnp.int32, (ROW_SUBLANES, LANES), 0)
    masks = [(4, (sub & 4) == 0), (2, (sub & 2) == 0), (1, (sub & 1) == 0)]
    sh = (sh0_ref, sh1_ref)
    part = (part0_ref, part1_ref)
    last = PEER_TOKEN_TILE - 1

    def lane_sums(src_ref, slot, t):
        s_ref[pl.ds(t, 1), :] = jnp.sum(src_ref[slot].T, axis=0, keepdims=True)

    for u in range(BLOCK_TOKENS):
        _replicate(shift_ref, u, sh0_ref, u)
    part1_ref[...] = jnp.zeros(part1_ref.shape, jnp.float32)

    def step(t, slot, cur, nxt):
        x = x_ref[t]
        for g in range(PEER_PICKS // ROW_SUBLANES):
            picks = range(g * ROW_SUBLANES, (g + 1) * ROW_SUBLANES)
            prods = [_expert_row(utab_ref, row_ref[t, k],
                                 _splat_row(sh[cur], slot, k).astype(jnp.uint32)) * x
                     for k in picks]
            part[cur][slot, pl.ds(g * ROW_SUBLANES, ROW_SUBLANES), :] = _sublane_sums(prods, masks)
        _replicate(shift_ref, jnp.minimum(t + BLOCK_TOKENS, last), sh[nxt], slot)
        lane_sums(part[nxt], slot, jnp.maximum(t - BLOCK_TOKENS, 0))

    _two_buffer_token_loop(step)
    for u in range(BLOCK_TOKENS):
        lane_sums(part1_ref, u, PEER_TOKEN_TILE - BLOCK_TOKENS + u)
    act_ref[...] = _pack_act(_gelu_exact(s_ref[...]) * gate_ref[...], shift_ref[...])


def _peer_out_body(row_ref, act_ref, vtab_ref, y_ref, pk0_ref, pk1_ref):
    pk = (pk0_ref, pk1_ref)
    last = PEER_TOKEN_TILE - 1
    for u in range(BLOCK_TOKENS):
        _replicate(act_ref, u, pk0_ref, u)

    def step(t, slot, cur, nxt):
        accs = [jnp.zeros((ROW_SUBLANES, LANES), jnp.float32) for _ in range(OUT_PARTIAL_SUMS)]
        for k in range(PEER_PICKS):
            v = _splat_row(pk[cur], slot, k)
            a = lax.bitcast_convert_type(v & jnp.uint32(HIGH_HALF_MASK), jnp.float32)
            r = _expert_row(vtab_ref, row_ref[t, k], v & jnp.uint32(SHIFT_FIELD_MASK))
            accs[k % OUT_PARTIAL_SUMS] = accs[k % OUT_PARTIAL_SUMS] + a * r
        y_ref[t] = (accs[0] + accs[1]) + (accs[2] + accs[3])
        _replicate(act_ref, jnp.minimum(t + BLOCK_TOKENS, last), pk[nxt], slot)

    _two_buffer_token_loop(step)


def _peer_experts(xn, rows8, shifts, gates, expert_u, expert_v):
    T, d = xn.shape
    tt = PEER_TOKEN_TILE
    assert T % tt == 0 and d == ROW_SUBLANES * LANES and ROW_SUBLANES == VREG_SUBLANES
    assert tt % (2 * BLOCK_TOKENS) == 0
    x3 = xn.reshape(T, ROW_SUBLANES, LANES)
    smem_spec = pl.BlockSpec((tt, PEER_PICKS), lambda i: (i, 0), memory_space=pltpu.SMEM)
    pick_spec = pl.BlockSpec((tt, PEER_PICKS), lambda i: (i, 0))
    tok_spec = pl.BlockSpec((tt, ROW_SUBLANES, LANES), lambda i: (i, 0, 0))
    table_spec = pl.BlockSpec(memory_space=pltpu.VMEM)
    params = pltpu.CompilerParams(dimension_semantics=("arbitrary",),
                                  vmem_limit_bytes=PEER_VMEM_LIMIT)

    def replicated(dtype):
        return pltpu.VMEM((BLOCK_TOKENS, PEER_PICKS, LANES), dtype)

    act = pl.pallas_call(
        _peer_hidden_body,
        grid=(T // tt,),
        in_specs=[smem_spec, pick_spec, tok_spec, pick_spec, table_spec],
        out_specs=pick_spec,
        out_shape=jax.ShapeDtypeStruct((T, PEER_PICKS), jnp.uint32),
        scratch_shapes=[replicated(jnp.int32), replicated(jnp.int32),
                        replicated(jnp.float32), replicated(jnp.float32),
                        pltpu.VMEM((tt, PEER_PICKS), jnp.float32)],
        compiler_params=params,
        name="peer_hidden",
    )(rows8, shifts, x3, gates, _pack_expert_table(expert_u))
    y = pl.pallas_call(
        _peer_out_body,
        grid=(T // tt,),
        in_specs=[smem_spec, pick_spec, table_spec],
        out_specs=tok_spec,
        out_shape=jax.ShapeDtypeStruct((T, ROW_SUBLANES, LANES), jnp.float32),
        scratch_shapes=[replicated(jnp.uint32), replicated(jnp.uint32)],
        compiler_params=params,
        name="peer_out",
    )(rows8, act, _pack_expert_table(expert_v))
    return y.reshape(T, d)


def _peer_ffn(h, g_ffn, w_q, sub_keys, expert_u, expert_v):
    xn, rows8, shifts, gates = _peer_route(h, g_ffn, w_q, sub_keys)
    return _peer_experts(xn, rows8, shifts, gates, expert_u, expert_v)


def _final_norm_body(h_ref, f_ref, g_ref, o_ref):
    xf = h_ref[...] + f_ref[...]
    ms = jnp.mean(xf * xf, axis=-1, keepdims=True)
    o_ref[...] = xf * lax.rsqrt(ms + EPS) * g_ref[...]


def _final_norm(h, ffn, g):
    rows, d = h.shape
    tile = 512
    return pl.pallas_call(
        _final_norm_body,
        grid=(rows // tile,),
        in_specs=[pl.BlockSpec((tile, d), lambda i: (i, 0)),
                  pl.BlockSpec((tile, d), lambda i: (i, 0)),
                  pl.BlockSpec((1, d), lambda i: (0, 0))],
        out_specs=pl.BlockSpec((tile, d), lambda i: (i, 0)),
        out_shape=jax.ShapeDtypeStruct((rows, d), jnp.float32),
        name="final_norm",
    )(h, ffn, g.reshape(1, d))


def kernel(x, meta_tokens, g_mix, w_in, conv_ssd_w, conv_ssd_b, dt_bias, a_log, d_skip,
           ssd_norm_w, conv_sc_w, w_out, g_ffn, w_q, sub_keys, expert_u, expert_v, g_final):
    bsz, seq, d = x.shape
    assert g_mix.shape[0] == 1
    h = _mixer(x, meta_tokens, g_mix[0], w_in[0], conv_ssd_w[0], conv_ssd_b[0], dt_bias[0],
               a_log[0], d_skip[0], ssd_norm_w[0], conv_sc_w[0], w_out[0])
    ffn = _peer_ffn(h, g_ffn[0], w_q[0], sub_keys[0], expert_u[0], expert_v[0])
    return _final_norm(h, ffn, g_final).reshape(bsz, seq, d)
```

```python
import numpy as np

import jax
import jax.numpy as jnp
from jax import lax
from jax.experimental import pallas as pl
from jax.experimental.pallas import tpu as pltpu

D_MODEL = 1024
N_META = 16
D_MIX = 2 * D_MODEL
SSD_HEAD_DIM = 64
SSD_HEADS = 24
D_SSD = SSD_HEADS * SSD_HEAD_DIM
SSD_GROUPS = 4
SSD_STATE = 128
SSD_CONV = 4
SSD_CHUNK = 128
D_XBC = D_SSD + 2 * SSD_GROUPS * SSD_STATE
D_SC = D_MIX - D_SSD
SC_CONV = 3
D_IN = D_SSD + D_XBC + SSD_HEADS + 3 * D_SC
PEER_HEADS = 8
PEER_KEYS = 128
PEER_TOPK = 16
PEER_QDIM = 256
PEER_HALF = PEER_QDIM // 2
EPS = 1e-6

LANES = 128
VREG_SUBLANES = 8


SSD_PAD = SSD_CHUNK - N_META
HEADS_PER_GROUP = SSD_HEADS // SSD_GROUPS
D_GROUP = HEADS_PER_GROUP * SSD_HEAD_DIM
D_BC = SSD_GROUPS * SSD_STATE
D_SC3 = 3 * D_SC
D_IN_ALIGNED = D_SSD + D_XBC + LANES + D_SC3
INPROJ_TOKEN_TILE = 256
INPROJ_COL_TILE = 512
HALO_ROWS = VREG_SUBLANES


def _inproj_body(h_ref, g_ref, w_ref, z_ref, xbc_ref, dt_ref, sc_ref):
    hx = h_ref[...]
    xn = hx * lax.rsqrt(jnp.mean(hx * hx, axis=-1, keepdims=True) + EPS) * g_ref[...]
    xb = xn.astype(jnp.bfloat16)
    col = 0
    for out_ref in (z_ref, xbc_ref, dt_ref, sc_ref):
        width = out_ref.shape[1]
        for c0 in range(0, width, INPROJ_COL_TILE):
            cw = min(INPROJ_COL_TILE, width - c0)
            out_ref[:, c0:c0 + cw] = jnp.dot(xb, w_ref[:, col + c0:col + c0 + cw],
                                             preferred_element_type=jnp.float32)
        col += width


def _inproj(h_pad, g_mix, w_in):
    rows, d = h_pad.shape
    tm = INPROJ_TOKEN_TILE
    assert rows % tm == 0
    s1, s2, s3 = D_SSD, D_SSD + D_XBC, D_SSD + D_XBC + SSD_HEADS
    w = jnp.concatenate([w_in[:, :s2], w_in[:, s2:s3],
                         jnp.zeros((d, LANES - SSD_HEADS), w_in.dtype), w_in[:, s3:]],
                        axis=1).astype(jnp.bfloat16)
    widths = (D_SSD, D_XBC, LANES, D_SC3)
    return pl.pallas_call(
        _inproj_body,
        grid=(rows // tm,),
        in_specs=[pl.BlockSpec((tm, d), lambda i: (i, 0)),
                  pl.BlockSpec((1, d), lambda i: (0, 0)),
                  pl.BlockSpec((d, D_IN_ALIGNED), lambda i: (0, 0))],
        out_specs=[pl.BlockSpec((tm, wd), lambda i: (i, 0)) for wd in widths],
        out_shape=[jax.ShapeDtypeStruct((rows, wd), jnp.float32) for wd in widths],
        compiler_params=pltpu.CompilerParams(dimension_semantics=("arbitrary",)),
        name="mixer_inproj",
    )(h_pad, g_mix.reshape(1, d), w)


def _silu(v):
    return v / (1.0 + jnp.exp(-v))


def _mixer_chunk_body(z_ref, xbc_ref, dt_ref, sc_ref, x_ref, cw_ref, cb_ref, dtb_ref, alog_ref,
                      dskip_ref, nw_ref, scw_ref, expand_ref, wout_ref, o_ref,
                      ext_ref, ext2_ref, state_ref):
    c = pl.program_id(1)
    q = SSD_CHUNK
    hi = lax.Precision.HIGHEST

    @pl.when(c == 0)
    def _():
        ext_ref[0:HALO_ROWS, :] = jnp.zeros((HALO_ROWS, D_XBC), jnp.float32)
        ext2_ref[0:HALO_ROWS, :] = jnp.zeros((HALO_ROWS, D_SC), jnp.float32)
        state_ref[...] = jnp.zeros(state_ref.shape, jnp.float32)

    row = lax.broadcasted_iota(jnp.int32, (q, 1), 0)
    live = jnp.logical_or(c > 0, row >= SSD_PAD)

    ext_ref[HALO_ROWS:HALO_ROWS + q, :] = xbc_ref[...]
    conv = jnp.broadcast_to(cb_ref[...], (q, D_XBC))
    for j in range(SSD_CONV):
        conv = conv + cw_ref[j:j + 1, :] * ext_ref[pl.ds(HALO_ROWS - (SSD_CONV - 1) + j, q), :]
    ext_ref[0:HALO_ROWS, :] = ext_ref[q:q + HALO_ROWS, :]
    xbc = jnp.where(live, _silu(conv), 0.0)
    xs = xbc[:, :D_SSD]
    bm = xbc[:, D_SSD:D_SSD + D_BC]
    cm = xbc[:, D_SSD + D_BC:]

    dtr = dt_ref[...] + dtb_ref[...]
    dt = jnp.maximum(dtr, 0.0) + jnp.log1p(jnp.exp(-jnp.abs(dtr)))
    dt = jnp.where(live, dt, 0.0)
    a = dt * (-jnp.exp(alog_ref[...]))
    tri = (lax.broadcasted_iota(jnp.int32, (q, q), 0)
           >= lax.broadcasted_iota(jnp.int32, (q, q), 1))
    a_cs = jnp.dot(tri.astype(jnp.float32), a, precision=hi,
                   preferred_element_type=jnp.float32)
    a_cs_t = a_cs.T
    a_last = a_cs[q - 1:q, :]
    expand = expand_ref[...]
    dt_w = jnp.dot(dt, expand, precision=hi, preferred_element_type=jnp.float32)
    acs_w = jnp.dot(a_cs, expand, precision=hi, preferred_element_type=jnp.float32)
    alast_w = jnp.dot(a_last, expand, precision=hi, preferred_element_type=jnp.float32)
    x_dt = xs * dt_w
    grow = jnp.exp(acs_w)
    x_tail = (x_dt * jnp.exp(alast_w - acs_w)).astype(jnp.bfloat16)
    x_dt_b = x_dt.astype(jnp.bfloat16)
    lane = lax.broadcasted_iota(jnp.int32, (q, LANES), 1)

    y_groups = []
    for g in range(SSD_GROUPS):
        gs = slice(g * D_GROUP, (g + 1) * D_GROUP)
        bg = bm[:, g * SSD_STATE:(g + 1) * SSD_STATE]
        cg = cm[:, g * SSD_STATE:(g + 1) * SSD_STATE].astype(jnp.bfloat16)
        cb = lax.dot_general(cg, bg.astype(jnp.bfloat16), (((1,), (1,)), ((), ())),
                             preferred_element_type=jnp.float32)
        state = state_ref[g]
        y_off = jnp.dot(cg, state.astype(jnp.bfloat16),
                        preferred_element_type=jnp.float32) * grow[:, gs]
        y_pairs = []
        for pair in range(HEADS_PER_GROUP // 2):
            cols = slice(g * D_GROUP + pair * LANES, g * D_GROUP + (pair + 1) * LANES)
            xp = x_dt_b[:, cols]
            halves = []
            for sub in range(2):
                hd = g * HEADS_PER_GROUP + 2 * pair + sub
                seg = (jnp.broadcast_to(a_cs[:, hd:hd + 1], (q, q))
                       - jnp.broadcast_to(a_cs_t[hd:hd + 1, :], (q, q)))
                decay = jnp.exp(jnp.where(tri, seg, -jnp.inf))
                halves.append(jnp.dot((cb * decay).astype(jnp.bfloat16), xp,
                                      preferred_element_type=jnp.float32))
            y_pairs.append(jnp.where(lane < SSD_HEAD_DIM, halves[0], halves[1]))
        y_groups.append(jnp.concatenate(y_pairs, axis=1) + y_off)
        bg_t = bg.T.astype(jnp.bfloat16)
        state_ref[g] = (state * jnp.exp(alast_w[:, gs])
                        + jnp.dot(bg_t, x_tail[:, gs], preferred_element_type=jnp.float32))

    y = jnp.concatenate(y_groups, axis=1) + dskip_ref[...] * xs
    y = y * _silu(z_ref[...])
    normed = []
    for g in range(SSD_GROUPS):
        yg = y[:, g * D_GROUP:(g + 1) * D_GROUP]
        ms = jnp.mean(yg * yg, axis=-1, keepdims=True)
        normed.append(yg * lax.rsqrt(ms + EPS))
    y_ssd = jnp.concatenate(normed, axis=1) * nw_ref[...]

    sc = sc_ref[...]
    ext2_ref[HALO_ROWS:HALO_ROWS + q, :] = sc[:, D_SC:2 * D_SC] * sc[:, 2 * D_SC:]
    conv2 = jnp.zeros((q, D_SC), jnp.float32)
    for j in range(SC_CONV):
        conv2 = conv2 + scw_ref[j:j + 1, :] * ext2_ref[pl.ds(HALO_ROWS - (SC_CONV - 1) + j, q), :]
    ext2_ref[0:HALO_ROWS, :] = ext2_ref[q:q + HALO_ROWS, :]
    y_sc = sc[:, :D_SC] * conv2

    @pl.when(c > 0)
    def _():
        ycat = jnp.concatenate([y_ssd, y_sc], axis=1).astype(jnp.bfloat16)
        o_ref[...] = x_ref[...] + jnp.dot(ycat, wout_ref[...],
                                          preferred_element_type=jnp.float32)


def _head_expand_matrix():
    e = np.zeros((LANES, D_SSD), np.float32)
    for hd in range(SSD_HEADS):
        e[hd, hd * SSD_HEAD_DIM:(hd + 1) * SSD_HEAD_DIM] = 1.0
    return jnp.asarray(e)


def _mixer(x, meta_tokens, g_mix, w_in, conv_ssd_w, conv_ssd_b, dt_bias, a_log, d_skip,
           ssd_norm_w, conv_sc_w, w_out):
    bsz, seq, d = x.shape
    assert seq % SSD_CHUNK == 0
    n_chunks = seq // SSD_CHUNK + 1
    meta = jnp.broadcast_to(meta_tokens[None].astype(x.dtype), (bsz, N_META, d))
    h_pad = jnp.concatenate([jnp.zeros((bsz, SSD_PAD, d), x.dtype), meta, x], axis=1)
    z, xbc, dt_raw, sc = _inproj(h_pad.reshape(bsz * n_chunks * SSD_CHUNK, d), g_mix, w_in)

    def pad_heads(v):
        return jnp.pad(v.astype(jnp.float32), (0, LANES - SSD_HEADS)).reshape(1, LANES)

    params = [conv_ssd_w, conv_ssd_b.reshape(1, D_XBC), pad_heads(dt_bias), pad_heads(a_log),
              jnp.repeat(d_skip.astype(jnp.float32), SSD_HEAD_DIM).reshape(1, D_SSD),
              ssd_norm_w.reshape(1, D_SSD), conv_sc_w, _head_expand_matrix(),
              w_out.astype(jnp.bfloat16)]

    def chunk_rows(width):
        return pl.BlockSpec((SSD_CHUNK, width), lambda b, c: (b * n_chunks + c, 0))

    def seq_rows(b, c):
        return (b * (n_chunks - 1) + jnp.maximum(c - 1, 0), 0)

    def whole(arr):
        return pl.BlockSpec(arr.shape, lambda b, c: (0,) * arr.ndim)

    return pl.pallas_call(
        _mixer_chunk_body,
        grid=(bsz, n_chunks),
        in_specs=[chunk_rows(D_SSD), chunk_rows(D_XBC), chunk_rows(LANES), chunk_rows(D_SC3),
                  pl.BlockSpec((SSD_CHUNK, d), seq_rows)] + [whole(p) for p in params],
        out_specs=pl.BlockSpec((SSD_CHUNK, d), seq_rows),
        out_shape=jax.ShapeDtypeStruct((bsz * seq, d), jnp.float32),
        scratch_shapes=[pltpu.VMEM((HALO_ROWS + SSD_CHUNK, D_XBC), jnp.float32),
                        pltpu.VMEM((HALO_ROWS + SSD_CHUNK, D_SC), jnp.float32),
                        pltpu.VMEM((SSD_GROUPS, SSD_STATE, D_GROUP), jnp.float32)],
        compiler_params=pltpu.CompilerParams(dimension_semantics=("arbitrary", "arbitrary")),
        name="mixer_chunk",
    )(z, xbc, dt_raw, sc, x.reshape(bsz * seq, d), *params)


PEER_PICKS = PEER_HEADS * PEER_TOPK
PACKED_ROWS = PEER_KEYS * PEER_KEYS // 2
ROUTE_TOKEN_TILE = 256

_STAIR = [(a, b) for a in range(PEER_TOPK) for b in range(PEER_TOPK)
          if (a + 1) * (b + 1) <= PEER_TOPK]
STAIR_ROWS = -(-len(_STAIR) // VREG_SUBLANES) * VREG_SUBLANES


def _stair_constants():
    sel_a = np.zeros((STAIR_ROWS, PEER_TOPK), np.float32)
    sel_b = np.zeros((STAIR_ROWS, PEER_TOPK), np.float32)
    pad = np.full((STAIR_ROWS, LANES), -np.inf, np.float32)
    for c, (a, b) in enumerate(_STAIR):
        sel_a[c, a] = 1.0
        sel_b[c, b] = 1.0
        pad[c, :] = 0.0
    return jnp.asarray(sel_a), jnp.asarray(sel_b), jnp.asarray(pad)


def _top16_rows(chains, val_refs, idx_refs):
    rows = chains[0].shape[0]
    iota = lax.broadcasted_iota(jnp.int32, chains[0].shape, 0).astype(jnp.float32)
    chains = list(chains)
    for r in range(PEER_TOPK):
        for n, c in enumerate(chains):
            m = jnp.max(c, axis=0, keepdims=True)
            idx = jnp.min(jnp.where(c == m, iota, float(rows)), axis=0, keepdims=True)
            val_refs[n][pl.ds(r, 1), :] = m
            idx_refs[n][pl.ds(r, 1), :] = idx
            chains[n] = jnp.where(iota == idx, -jnp.inf, c)


def _peer_route_body(h_ref, g_ref, wq_ref, keys_ref, sela_ref, selb_ref, pad_ref,
                     xn_ref, rows_ref, shift_ref, gate_ref,
                     q_ref, v_ref, i_ref, best_ref, pos_ref, e_ref, p_ref):
    hx = h_ref[...]
    xn = hx * lax.rsqrt(jnp.mean(hx * hx, axis=-1, keepdims=True) + EPS) * g_ref[...]
    xn_ref[...] = xn
    xb = xn.astype(jnp.bfloat16)
    for hh in range(PEER_HEADS):
        qh = jnp.dot(xb, wq_ref[:, hh * PEER_QDIM:(hh + 1) * PEER_QDIM],
                     preferred_element_type=jnp.float32).astype(jnp.bfloat16)
        q_ref[2 * hh] = qh[:, :PEER_HALF]
        q_ref[2 * hh + 1] = qh[:, PEER_HALF:]

    lane_tiles = range(ROUTE_TOKEN_TILE // LANES)
    hi = lax.Precision.HIGHEST

    def head(hh, carry):
        for lt in lane_tiles:
            scores = [lax.dot_general(keys_ref[2 * hh + half],
                                      q_ref[2 * hh + half, pl.ds(lt * LANES, LANES), :],
                                      (((1,), (1,)), ((), ())),
                                      preferred_element_type=jnp.float32)
                      for half in range(2)]
            _top16_rows(scores, [v_ref.at[lt, 0], v_ref.at[lt, 1]],
                        [i_ref.at[lt, 0], i_ref.at[lt, 1]])
        cands, cand_es = [], []
        for lt in lane_tiles:
            cands.append(
                jnp.dot(sela_ref[...], v_ref[lt, 0], precision=hi,
                        preferred_element_type=jnp.float32)
                + jnp.dot(selb_ref[...], v_ref[lt, 1], precision=hi,
                          preferred_element_type=jnp.float32)
                + pad_ref[...])
            ea = jnp.dot(sela_ref[...], i_ref[lt, 0], preferred_element_type=jnp.float32)
            eb = jnp.dot(selb_ref[...], i_ref[lt, 1], preferred_element_type=jnp.float32)
            cand_es.append((ea * PEER_KEYS + eb).astype(jnp.int32))
        _top16_rows(cands, [best_ref.at[lt] for lt in lane_tiles],
                    [pos_ref.at[lt] for lt in lane_tiles])
        crow = lax.broadcasted_iota(jnp.int32, cands[0].shape, 0).astype(jnp.float32)
        base = pl.multiple_of(hh * PEER_TOPK, PEER_TOPK)
        for lt in lane_tiles:
            best = best_ref[lt]
            pos = pos_ref[lt]
            picks = [jnp.sum(jnp.where(crow == pos[r:r + 1, :], cand_es[lt], 0),
                             axis=0, keepdims=True)
                     for r in range(PEER_TOPK)]
            w = jnp.exp(best - best[0:1, :])
            p_ref[lt, pl.ds(base, PEER_TOPK), :] = w / jnp.sum(w, axis=0, keepdims=True)
            e_ref[lt, pl.ds(base, PEER_TOPK), :] = jnp.concatenate(picks, axis=0)
        return carry

    lax.fori_loop(0, PEER_HEADS, head, 0)
    for lt in lane_tiles:
        tok = pl.ds(lt * LANES, LANES)
        e_t = e_ref[lt].T
        rows_ref[tok, :] = (e_t & (PACKED_ROWS - 1)) * VREG_SUBLANES
        shift_ref[tok, :] = (1 - (e_t >> 13)) * 16
        gate_ref[tok, :] = p_ref[lt].T


def _peer_route(h, g_ffn, w_q, sub_keys):
    T, d = h.shape
    tt = ROUTE_TOKEN_TILE
    assert T % tt == 0 and PACKED_ROWS == 1 << 13
    lts = tt // LANES
    sel_a, sel_b, pad = _stair_constants()
    keys = sub_keys.reshape(PEER_HEADS * 2, PEER_KEYS, PEER_HALF).astype(jnp.bfloat16)
    tok_spec = pl.BlockSpec((tt, d), lambda i: (i, 0))
    pick_spec = pl.BlockSpec((tt, PEER_PICKS), lambda i: (i, 0))

    def whole(shape):
        return pl.BlockSpec(shape, lambda i: (0,) * len(shape))

    return pl.pallas_call(
        _peer_route_body,
        grid=(T // tt,),
        in_specs=[tok_spec, whole((1, d)), whole((d, PEER_HEADS * PEER_QDIM)),
                  whole(keys.shape), whole(sel_a.shape), whole(sel_b.shape), whole(pad.shape)],
        out_specs=[tok_spec, pick_spec, pick_spec, pick_spec],
        out_shape=[jax.ShapeDtypeStruct((T, d), jnp.float32),
                   jax.ShapeDtypeStruct((T, PEER_PICKS), jnp.int32),
                   jax.ShapeDtypeStruct((T, PEER_PICKS), jnp.int32),
                   jax.ShapeDtypeStruct((T, PEER_PICKS), jnp.float32)],
        scratch_shapes=[pltpu.VMEM((2 * PEER_HEADS, tt, PEER_HALF), jnp.bfloat16),
                        pltpu.VMEM((lts, 2, PEER_TOPK, LANES), jnp.float32),
                        pltpu.VMEM((lts, 2, PEER_TOPK, LANES), jnp.float32),
                        pltpu.VMEM((lts, PEER_TOPK, LANES), jnp.float32),
                        pltpu.VMEM((lts, PEER_TOPK, LANES), jnp.float32),
                        pltpu.VMEM((lts, PEER_PICKS, LANES), jnp.int32),
                        pltpu.VMEM((lts, PEER_PICKS, LANES), jnp.float32)],
        compiler_params=pltpu.CompilerParams(dimension_semantics=("arbitrary",)),
        name="peer_route",
    )(h, g_ffn.reshape(1, d), w_q.astype(jnp.bfloat16), keys, sel_a, sel_b, pad)


ROW_SUBLANES = D_MODEL // LANES
PEER_TOKEN_TILE = 128
PACKED_TABLE_BYTES = PACKED_ROWS * D_MODEL * 4
BLOCK_TOKENS = 8
LANE_REPLICATED_BYTES = BLOCK_TOKENS * PEER_PICKS * LANES * 4
PEER_VMEM_LIMIT = PACKED_TABLE_BYTES + 4 * LANE_REPLICATED_BYTES + 8 * 1024 * 1024
HIGH_HALF_MASK = 0xFFFF0000
SHIFT_FIELD_MASK = 0x1F
_BITREV8 = (0, 4, 2, 6, 1, 5, 3, 7)


def _pack_expert_table(tab):
    bits = lax.bitcast_convert_type(tab.astype(jnp.bfloat16), jnp.uint16).astype(jnp.uint32)
    packed = bits[:PACKED_ROWS] | (bits[PACKED_ROWS:] << 16)
    return packed.reshape(PACKED_ROWS * ROW_SUBLANES, LANES)


def _gelu_exact(s):
    return 0.5 * s * (1.0 + lax.erf(s * (2.0 ** -0.5)))


OUT_PARTIAL_SUMS = 4


def _replicate(src_ref, t, dst_ref, slot):
    dst_ref[slot] = jnp.broadcast_to(src_ref[pl.ds(t, 1), :], (PEER_PICKS, LANES)).T


def _splat_row(ref, slot, k):
    return jnp.broadcast_to(ref[slot, pl.ds(k, 1), :], (ROW_SUBLANES, LANES))


def _expert_row(tab_ref, row8, shift):
    w = tab_ref[pl.ds(pl.multiple_of(row8, ROW_SUBLANES), ROW_SUBLANES), :]
    return lax.bitcast_convert_type((w << shift) & jnp.uint32(HIGH_HALF_MASK), jnp.float32)


def _pack_act(act, shift):
    bits = lax.bitcast_convert_type(act, jnp.uint32)
    rounded = bits + jnp.uint32(0x7FFF) + ((bits >> 16) & jnp.uint32(1))
    return (rounded & jnp.uint32(HIGH_HALF_MASK)) | shift.astype(jnp.uint32)


def _fold_pair(a, b, shift, keep_a):
    if 2 * shift == ROW_SUBLANES:
        return jnp.where(keep_a, a, b) + pltpu.roll(jnp.where(keep_a, b, a), shift, axis=0)
    return (jnp.where(keep_a, a, pltpu.roll(b, shift, axis=0))
            + jnp.where(keep_a, pltpu.roll(a, ROW_SUBLANES - shift, axis=0), b))


def _sublane_sums(tiles, masks):
    level = [tiles[_BITREV8[s]] for s in range(ROW_SUBLANES)]
    for shift, keep in masks:
        level = [_fold_pair(level[2 * i], level[2 * i + 1], shift, keep)
                 for i in range(len(level) // 2)]
    return level[0]


def _two_buffer_token_loop(step):
    def block_pair(i, carry):
        base = i * (2 * BLOCK_TOKENS)
        for cur in range(2):
            def one(u, c, cur=cur):
                step(base + cur * BLOCK_TOKENS + u, u, cur, 1 - cur)
                return c

            lax.fori_loop(0, BLOCK_TOKENS, one, 0)
        return carry

    lax.fori_loop(0, PEER_TOKEN_TILE // (2 * BLOCK_TOKENS), block_pair, 0)


def _peer_hidden_body(row_ref, shift_ref, x_ref, gate_ref, utab_ref, act_ref,
                      sh0_ref, sh1_ref, part0_ref, part1_ref, s_ref):
    sub = lax.broadcasted_iota(jnp.int32, (ROW_SUBLANES, LANES), 0)
    masks = [(4, (sub & 4) == 0), (2, (sub & 2) == 0), (1, (sub & 1) == 0)]
    sh = (sh0_ref, sh1_ref)
    part = (part0_ref, part1_ref)
    last = PEER_TOKEN_TILE - 1

    def lane_sums(src_ref, slot, t):
        s_ref[pl.ds(t, 1), :] = jnp.sum(src_ref[slot].T, axis=0, keepdims=True)

    for u in range(BLOCK_TOKENS):
        _replicate(shift_ref, u, sh0_ref, u)
    part1_ref[...] = jnp.zeros(part1_ref.shape, jnp.float32)

    def step(t, slot, cur, nxt):
        x = x_ref[t]
        for g in range(PEER_PICKS // ROW_SUBLANES):
            picks = range(g * ROW_SUBLANES, (g + 1) * ROW_SUBLANES)
            prods = [_expert_row(utab_ref, row_ref[t, k],
                                 _splat_row(sh[cur], slot, k).astype(jnp.uint32)) * x
                     for k in picks]
            part[cur][slot, pl.ds(g * ROW_SUBLANES, ROW_SUBLANES), :] = _sublane_sums(prods, masks)
        _replicate(shift_ref, jnp.minimum(t + BLOCK_TOKENS, last), sh[nxt], slot)
        lane_sums(part[nxt], slot, jnp.maximum(t - BLOCK_TOKENS, 0))

    _two_buffer_token_loop(step)
    for u in range(BLOCK_TOKENS):
        lane_sums(part1_ref, u, PEER_TOKEN_TILE - BLOCK_TOKENS + u)
    act_ref[...] = _pack_act(_gelu_exact(s_ref[...]) * gate_ref[...], shift_ref[...])


def _peer_out_body(row_ref, act_ref, vtab_ref, y_ref, pk0_ref, pk1_ref):
    pk = (pk0_ref, pk1_ref)
    last = PEER_TOKEN_TILE - 1
    for u in range(BLOCK_TOKENS):
        _replicate(act_ref, u, pk0_ref, u)

    def step(t, slot, cur, nxt):
        accs = [jnp.zeros((ROW_SUBLANES, LANES), jnp.float32) for _ in range(OUT_PARTIAL_SUMS)]
        for k in range(PEER_PICKS):
            v = _splat_row(pk[cur], slot, k)
            a = lax.bitcast_convert_type(v & jnp.uint32(HIGH_HALF_MASK), jnp.float32)
            r = _expert_row(vtab_ref, row_ref[t, k], v & jnp.uint32(SHIFT_FIELD_MASK))
            accs[k % OUT_PARTIAL_SUMS] = accs[k % OUT_PARTIAL_SUMS] + a * r
        y_ref[t] = (accs[0] + accs[1]) + (accs[2] + accs[3])
        _replicate(act_ref, jnp.minimum(t + BLOCK_TOKENS, last), pk[nxt], slot)

    _two_buffer_token_loop(step)


def _peer_experts(xn, rows8, shifts, gates, expert_u, expert_v):
    T, d = xn.shape
    tt = PEER_TOKEN_TILE
    assert T % tt == 0 and d == ROW_SUBLANES * LANES and ROW_SUBLANES == VREG_SUBLANES
    assert tt % (2 * BLOCK_TOKENS) == 0
    x3 = xn.reshape(T, ROW_SUBLANES, LANES)
    smem_spec = pl.BlockSpec((tt, PEER_PICKS), lambda i: (i, 0), memory_space=pltpu.SMEM)
    pick_spec = pl.BlockSpec((tt, PEER_PICKS), lambda i: (i, 0))
    tok_spec = pl.BlockSpec((tt, ROW_SUBLANES, LANES), lambda i: (i, 0, 0))
    table_spec = pl.BlockSpec(memory_space=pltpu.VMEM)
    params = pltpu.CompilerParams(dimension_semantics=("arbitrary",),
                                  vmem_limit_bytes=PEER_VMEM_LIMIT)

    def replicated(dtype):
        return pltpu.VMEM((BLOCK_TOKENS, PEER_PICKS, LANES), dtype)

    act = pl.pallas_call(
        _peer_hidden_body,
        grid=(T // tt,),
        in_specs=[smem_spec, pick_spec, tok_spec, pick_spec, table_spec],
        out_specs=pick_spec,
        out_shape=jax.ShapeDtypeStruct((T, PEER_PICKS), jnp.uint32),
        scratch_shapes=[replicated(jnp.int32), replicated(jnp.int32),
                        replicated(jnp.float32), replicated(jnp.float32),
                        pltpu.VMEM((tt, PEER_PICKS), jnp.float32)],
        compiler_params=params,
        name="peer_hidden",
    )(rows8, shifts, x3, gates, _pack_expert_table(expert_u))
    y = pl.pallas_call(
        _peer_out_body,
        grid=(T // tt,),
        in_specs=[smem_spec, pick_spec, table_spec],
        out_specs=tok_spec,
        out_shape=jax.ShapeDtypeStruct((T, ROW_SUBLANES, LANES), jnp.float32),
        scratch_shapes=[replicated(jnp.uint32), replicated(jnp.uint32)],
        compiler_params=params,
        name="peer_out",
    )(rows8, act, _pack_expert_table(expert_v))
    return y.reshape(T, d)


def _peer_ffn(h, g_ffn, w_q, sub_keys, expert_u, expert_v):
    xn, rows8, shifts, gates = _peer_route(h, g_ffn, w_q, sub_keys)
    return _peer_experts(xn, rows8, shifts, gates, expert_u, expert_v)


def _final_norm_body(h_ref, f_ref, g_ref, o_ref):
    xf = h_ref[...] + f_ref[...]
    ms = jnp.mean(xf * xf, axis=-1, keepdims=True)
    o_ref[...] = xf * lax.rsqrt(ms + EPS) * g_ref[...]


def _final_norm(h, ffn, g):
    rows, d = h.shape
    tile = 512
    return pl.pallas_call(
        _final_norm_body,
        grid=(rows // tile,),
        in_specs=[pl.BlockSpec((tile, d), lambda i: (i, 0)),
                  pl.BlockSpec((tile, d), lambda i: (i, 0)),
                  pl.BlockSpec((1, d), lambda i: (0, 0))],
        out_specs=pl.BlockSpec((tile, d), lambda i: (i, 0)),
        out_shape=jax.ShapeDtypeStruct((rows, d), jnp.float32),
        name="final_norm",
    )(h, ffn, g.reshape(1, d))


def kernel(x, meta_tokens, g_mix, w_in, conv_ssd_w, conv_ssd_b, dt_bias, a_log, d_skip,
           ssd_norm_w, conv_sc_w, w_out, g_ffn, w_q, sub_keys, expert_u, expert_v, g_final):
    bsz, seq, d = x.shape
    assert g_mix.shape[0] == 1
    h = _mixer(x, meta_tokens, g_mix[0], w_in[0], conv_ssd_w[0], conv_ssd_b[0], dt_bias[0],
               a_log[0], d_skip[0], ssd_norm_w[0], conv_sc_w[0], w_out[0])
    ffn = _peer_ffn(h, g_ffn[0], w_q[0], sub_keys[0], expert_u[0], expert_v[0])
    return _final_norm(h, ffn, g_final).reshape(bsz, seq, d)
```

```python
import numpy as np

import jax
import jax.numpy as jnp
from jax import lax
from jax.experimental import pallas as pl
from jax.experimental.pallas import tpu as pltpu

D_MODEL = 1024
N_META = 16
D_MIX = 2 * D_MODEL
SSD_HEAD_DIM = 64
SSD_HEADS = 24
D_SSD = SSD_HEADS * SSD_HEAD_DIM
SSD_GROUPS = 4
SSD_STATE = 128
SSD_CONV = 4
SSD_CHUNK = 128
D_XBC = D_SSD + 2 * SSD_GROUPS * SSD_STATE
D_SC = D_MIX - D_SSD
SC_CONV = 3
D_IN = D_SSD + D_XBC + SSD_HEADS + 3 * D_SC
PEER_HEADS = 8
PEER_KEYS = 128
PEER_TOPK = 16
PEER_QDIM = 256
PEER_HALF = PEER_QDIM // 2
EPS = 1e-6

LANES = 128
VREG_SUBLANES = 8


SSD_PAD = SSD_CHUNK - N_META
HEADS_PER_GROUP = SSD_HEADS // SSD_GROUPS
D_GROUP = HEADS_PER_GROUP * SSD_HEAD_DIM
D_BC = SSD_GROUPS * SSD_STATE
D_SC3 = 3 * D_SC
D_IN_ALIGNED = D_SSD + D_XBC + LANES + D_SC3
INPROJ_TOKEN_TILE = 256
INPROJ_COL_TILE = 512
HALO_ROWS = VREG_SUBLANES


def _inproj_body(h_ref, g_ref, w_ref, z_ref, xbc_ref, dt_ref, sc_ref):
    hx = h_ref[...]
    xn = hx * lax.rsqrt(jnp.mean(hx * hx, axis=-1, keepdims=True) + EPS) * g_ref[...]
    xb = xn.astype(jnp.bfloat16)
    col = 0
    for out_ref in (z_ref, xbc_ref, dt_ref, sc_ref):
        width = out_ref.shape[1]
        for c0 in range(0, width, INPROJ_COL_TILE):
            cw = min(INPROJ_COL_TILE, width - c0)
            out_ref[:, c0:c0 + cw] = jnp.dot(xb, w_ref[:, col + c0:col + c0 + cw],
                                             preferred_element_type=jnp.float32)
        col += width


def _inproj(h_pad, g_mix, w_in):
    rows, d = h_pad.shape
    tm = INPROJ_TOKEN_TILE
    assert rows % tm == 0
    s1, s2, s3 = D_SSD, D_SSD + D_XBC, D_SSD + D_XBC + SSD_HEADS
    w = jnp.concatenate([w_in[:, :s2], w_in[:, s2:s3],
                         jnp.zeros((d, LANES - SSD_HEADS), w_in.dtype), w_in[:, s3:]],
                        axis=1).astype(jnp.bfloat16)
    widths = (D_SSD, D_XBC, LANES, D_SC3)
    return pl.pallas_call(
        _inproj_body,
        grid=(rows // tm,),
        in_specs=[pl.BlockSpec((tm, d), lambda i: (i, 0)),
                  pl.BlockSpec((1, d), lambda i: (0, 0)),
                  pl.BlockSpec((d, D_IN_ALIGNED), lambda i: (0, 0))],
        out_specs=[pl.BlockSpec((tm, wd), lambda i: (i, 0)) for wd in widths],
        out_shape=[jax.ShapeDtypeStruct((rows, wd), jnp.float32) for wd in widths],
        compiler_params=pltpu.CompilerParams(dimension_semantics=("arbitrary",)),
        name="mixer_inproj",
    )(h_pad, g_mix.reshape(1, d), w)


def _silu(v):
    return v / (1.0 + jnp.exp(-v))


def _mixer_chunk_body(z_ref, xbc_ref, dt_ref, sc_ref, x_ref, cw_ref, cb_ref, dtb_ref, alog_ref,
                      dskip_ref, nw_ref, scw_ref, expand_ref, wout_ref, o_ref,
                      ext_ref, ext2_ref, state_ref):
    c = pl.program_id(1)
    q = SSD_CHUNK
    hi = lax.Precision.HIGHEST

    @pl.when(c == 0)
    def _():
        ext_ref[0:HALO_ROWS, :] = jnp.zeros((HALO_ROWS, D_XBC), jnp.float32)
        ext2_ref[0:HALO_ROWS, :] = jnp.zeros((HALO_ROWS, D_SC), jnp.float32)
        state_ref[...] = jnp.zeros(state_ref.shape, jnp.float32)

    row = lax.broadcasted_iota(jnp.int32, (q, 1), 0)
    live = jnp.logical_or(c > 0, row >= SSD_PAD)

    ext_ref[HALO_ROWS:HALO_ROWS + q, :] = xbc_ref[...]
    conv = jnp.broadcast_to(cb_ref[...], (q, D_XBC))
    for j in range(SSD_CONV):
        conv = conv + cw_ref[j:j + 1, :] * ext_ref[pl.ds(HALO_ROWS - (SSD_CONV - 1) + j, q), :]
    ext_ref[0:HALO_ROWS, :] = ext_ref[q:q + HALO_ROWS, :]
    xbc = jnp.where(live, _silu(conv), 0.0)
    xs = xbc[:, :D_SSD]
    bm = xbc[:, D_SSD:D_SSD + D_BC]
    cm = xbc[:, D_SSD + D_BC:]

    dtr = dt_ref[...] + dtb_ref[...]
    dt = jnp.maximum(dtr, 0.0) + jnp.log1p(jnp.exp(-jnp.abs(dtr)))
    dt = jnp.where(live, dt, 0.0)
    a = dt * (-jnp.exp(alog_ref[...]))
    tri = (lax.broadcasted_iota(jnp.int32, (q, q), 0)
           >= lax.broadcasted_iota(jnp.int32, (q, q), 1))
    a_cs = jnp.dot(tri.astype(jnp.float32), a, precision=hi,
                   preferred_element_type=jnp.float32)
    a_cs_t = a_cs.T
    a_last = a_cs[q - 1:q, :]
    expand = expand_ref[...]
    dt_w = jnp.dot(dt, expand, precision=hi, preferred_element_type=jnp.float32)
    acs_w = jnp.dot(a_cs, expand, precision=hi, preferred_element_type=jnp.float32)
    alast_w = jnp.dot(a_last, expand, precision=hi, preferred_element_type=jnp.float32)
    x_dt = xs * dt_w
    grow = jnp.exp(acs_w)
    x_tail = (x_dt * jnp.exp(alast_w - acs_w)).astype(jnp.bfloat16)
    x_dt_b = x_dt.astype(jnp.bfloat16)
    lane = lax.broadcasted_iota(jnp.int32, (q, LANES), 1)

    y_groups = []
    for g in range(SSD_GROUPS):
        gs = slice(g * D_GROUP, (g + 1) * D_GROUP)
        bg = bm[:, g * SSD_STATE:(g + 1) * SSD_STATE]
        cg = cm[:, g * SSD_STATE:(g + 1) * SSD_STATE].astype(jnp.bfloat16)
        cb = lax.dot_general(cg, bg.astype(jnp.bfloat16), (((1,), (1,)), ((), ())),
                             preferred_element_type=jnp.float32)
        state = state_ref[g]
        y_off = jnp.dot(cg, state.astype(jnp.bfloat16),
                        preferred_element_type=jnp.float32) * grow[:, gs]
        y_pairs = []
        for pair in range(HEADS_PER_GROUP // 2):
            cols = slice(g * D_GROUP + pair * LANES, g * D_GROUP + (pair + 1) * LANES)
            xp = x_dt_b[:, cols]
            halves = []
            for sub in range(2):
                hd = g * HEADS_PER_GROUP + 2 * pair + sub
                seg = (jnp.broadcast_to(a_cs[:, hd:hd + 1], (q, q))
                       - jnp.broadcast_to(a_cs_t[hd:hd + 1, :], (q, q)))
                decay = jnp.exp(jnp.where(tri, seg, -jnp.inf))
                halves.append(jnp.dot((cb * decay).astype(jnp.bfloat16), xp,
                                      preferred_element_type=jnp.float32))
            y_pairs.append(jnp.where(lane < SSD_HEAD_DIM, halves[0], halves[1]))
        y_groups.append(jnp.concatenate(y_pairs, axis=1) + y_off)
        bg_t = bg.T.astype(jnp.bfloat16)
        state_ref[g] = (state * jnp.exp(alast_w[:, gs])
                        + jnp.dot(bg_t, x_tail[:, gs], preferred_element_type=jnp.float32))

    y = jnp.concatenate(y_groups, axis=1) + dskip_ref[...] * xs
    y = y * _silu(z_ref[...])
    normed = []
    for g in range(SSD_GROUPS):
        yg = y[:, g * D_GROUP:(g + 1) * D_GROUP]
        ms = jnp.mean(yg * yg, axis=-1, keepdims=True)
        normed.append(yg * lax.rsqrt(ms + EPS))
    y_ssd = jnp.concatenate(normed, axis=1) * nw_ref[...]

    sc = sc_ref[...]
    ext2_ref[HALO_ROWS:HALO_ROWS + q, :] = sc[:, D_SC:2 * D_SC] * sc[:, 2 * D_SC:]
    conv2 = jnp.zeros((q, D_SC), jnp.float32)
    for j in range(SC_CONV):
        conv2 = conv2 + scw_ref[j:j + 1, :] * ext2_ref[pl.ds(HALO_ROWS - (SC_CONV - 1) + j, q), :]
    ext2_ref[0:HALO_ROWS, :] = ext2_ref[q:q + HALO_ROWS, :]
    y_sc = sc[:, :D_SC] * conv2

    @pl.when(c > 0)
    def _():
        ycat = jnp.concatenate([y_ssd, y_sc], axis=1).astype(jnp.bfloat16)
        o_ref[...] = x_ref[...] + jnp.dot(ycat, wout_ref[...],
                                          preferred_element_type=jnp.float32)


def _head_expand_matrix():
    e = np.zeros((LANES, D_SSD), np.float32)
    for hd in range(SSD_HEADS):
        e[hd, hd * SSD_HEAD_DIM:(hd + 1) * SSD_HEAD_DIM] = 1.0
    return jnp.asarray(e)


def _mixer(x, meta_tokens, g_mix, w_in, conv_ssd_w, conv_ssd_b, dt_bias, a_log, d_skip,
           ssd_norm_w, conv_sc_w, w_out):
    bsz, seq, d = x.shape
    assert seq % SSD_CHUNK == 0
    n_chunks = seq // SSD_CHUNK + 1
    meta = jnp.broadcast_to(meta_tokens[None].astype(x.dtype), (bsz, N_META, d))
    h_pad = jnp.concatenate([jnp.zeros((bsz, SSD_PAD, d), x.dtype), meta, x], axis=1)
    z, xbc, dt_raw, sc = _inproj(h_pad.reshape(bsz * n_chunks * SSD_CHUNK, d), g_mix, w_in)

    def pad_heads(v):
        return jnp.pad(v.astype(jnp.float32), (0, LANES - SSD_HEADS)).reshape(1, LANES)

    params = [conv_ssd_w, conv_ssd_b.reshape(1, D_XBC), pad_heads(dt_bias), pad_heads(a_log),
              jnp.repeat(d_skip.astype(jnp.float32), SSD_HEAD_DIM).reshape(1, D_SSD),
              ssd_norm_w.reshape(1, D_SSD), conv_sc_w, _head_expand_matrix(),
              w_out.astype(jnp.bfloat16)]

    def chunk_rows(width):
        return pl.BlockSpec((SSD_CHUNK, width), lambda b, c: (b * n_chunks + c, 0))

    def seq_rows(b, c):
        return (b * (n_chunks - 1) + jnp.maximum(c - 1, 0), 0)

    def whole(arr):
        return pl.BlockSpec(arr.shape, lambda b, c: (0,) * arr.ndim)

    return pl.pallas_call(
        _mixer_chunk_body,
        grid=(bsz, n_chunks),
        in_specs=[chunk_rows(D_SSD), chunk_rows(D_XBC), chunk_rows(LANES), chunk_rows(D_SC3),
                  pl.BlockSpec((SSD_CHUNK, d), seq_rows)] + [whole(p) for p in params],
        out_specs=pl.BlockSpec((SSD_CHUNK, d), seq_rows),
        out_shape=jax.ShapeDtypeStruct((bsz * seq, d), jnp.float32),
        scratch_shapes=[pltpu.VMEM((HALO_ROWS + SSD_CHUNK, D_XBC), jnp.float32),
                        pltpu.VMEM((HALO_ROWS + SSD_CHUNK, D_SC), jnp.float32),
                        pltpu.VMEM((SSD_GROUPS, SSD_STATE, D_GROUP), jnp.float32)],
        compiler_params=pltpu.CompilerParams(dimension_semantics=("arbitrary", "arbitrary")),
        name="mixer_chunk",
    )(z, xbc, dt_raw, sc, x.reshape(bsz * seq, d), *params)


PEER_PICKS = PEER_HEADS * PEER_TOPK
PACKED_ROWS = PEER_KEYS * PEER_KEYS // 2
ROUTE_TOKEN_TILE = 256

_STAIR = [(a, b) for a in range(PEER_TOPK) for b in range(PEER_TOPK)
          if (a + 1) * (b + 1) <= PEER_TOPK]
STAIR_ROWS = -(-len(_STAIR) // VREG_SUBLANES) * VREG_SUBLANES


def _stair_constants():
    sel_a = np.zeros((STAIR_ROWS, PEER_TOPK), np.float32)
    sel_b = np.zeros((STAIR_ROWS, PEER_TOPK), np.float32)
    pad = np.full((STAIR_ROWS, LANES), -np.inf, np.float32)
    for c, (a, b) in enumerate(_STAIR):
        sel_a[c, a] = 1.0
        sel_b[c, b] = 1.0
        pad[c, :] = 0.0
    return jnp.asarray(sel_a), jnp.asarray(sel_b), jnp.asarray(pad)


def _top16_rows(chains, val_refs, idx_refs):
    rows = chains[0].shape[0]
    iota = lax.broadcasted_iota(jnp.int32, chains[0].shape, 0).astype(jnp.float32)
    chains = list(chains)
    for r in range(PEER_TOPK):
        for n, c in enumerate(chains):
            m = jnp.max(c, axis=0, keepdims=True)
            idx = jnp.min(jnp.where(c == m, iota, float(rows)), axis=0, keepdims=True)
            val_refs[n][pl.ds(r, 1), :] = m
            idx_refs[n][pl.ds(r, 1), :] = idx
            chains[n] = jnp.where(iota == idx, -jnp.inf, c)


def _peer_route_body(h_ref, g_ref, wq_ref, keys_ref, sela_ref, selb_ref, pad_ref,
                     xn_ref, rows_ref, shift_ref, gate_ref,
                     q_ref, v_ref, i_ref, best_ref, pos_ref, e_ref, p_ref):
    hx = h_ref[...]
    xn = hx * lax.rsqrt(jnp.mean(hx * hx, axis=-1, keepdims=True) + EPS) * g_ref[...]
    xn_ref[...] = xn
    xb = xn.astype(jnp.bfloat16)
    for hh in range(PEER_HEADS):
        qh = jnp.dot(xb, wq_ref[:, hh * PEER_QDIM:(hh + 1) * PEER_QDIM],
                     preferred_element_type=jnp.float32).astype(jnp.bfloat16)
        q_ref[2 * hh] = qh[:, :PEER_HALF]
        q_ref[2 * hh + 1] = qh[:, PEER_HALF:]

    lane_tiles = range(ROUTE_TOKEN_TILE // LANES)
    hi = lax.Precision.HIGHEST

    def head(hh, carry):
        for lt in lane_tiles:
            scores = [lax.dot_general(keys_ref[2 * hh + half],
                                      q_ref[2 * hh + half, pl.ds(lt * LANES, LANES), :],
                                      (((1,), (1,)), ((), ())),
                                      preferred_element_type=jnp.float32)
                      for half in range(2)]
            _top16_rows(scores, [v_ref.at[lt, 0], v_ref.at[lt, 1]],
                        [i_ref.at[lt, 0], i_ref.at[lt, 1]])
        cands, cand_es = [], []
        for lt in lane_tiles:
            cands.append(
                jnp.dot(sela_ref[...], v_ref[lt, 0], precision=hi,
                        preferred_element_type=jnp.float32)
                + jnp.dot(selb_ref[...], v_ref[lt, 1], precision=hi,
                          preferred_element_type=jnp.float32)
                + pad_ref[...])
            ea = jnp.dot(sela_ref[...], i_ref[lt, 0], preferred_element_type=jnp.float32)
            eb = jnp.dot(selb_ref[...], i_ref[lt, 1], preferred_element_type=jnp.float32)
            cand_es.append((ea * PEER_KEYS + eb).astype(jnp.int32))
        _top16_rows(cands, [best_ref.at[lt] for lt in lane_tiles],
                    [pos_ref.at[lt] for lt in lane_tiles])
        crow = lax.broadcasted_iota(jnp.int32, cands[0].shape, 0).astype(jnp.float32)
        base = pl.multiple_of(hh * PEER_TOPK, PEER_TOPK)
        for lt in lane_tiles:
            best = best_ref[lt]
            pos = pos_ref[lt]
            picks = [jnp.sum(jnp.where(crow == pos[r:r + 1, :], cand_es[lt], 0),
                             axis=0, keepdims=True)
                     for r in range(PEER_TOPK)]
            w = jnp.exp(best - best[0:1, :])
            p_ref[lt, pl.ds(base, PEER_TOPK), :] = w / jnp.sum(w, axis=0, keepdims=True)
            e_ref[lt, pl.ds(base, PEER_TOPK), :] = jnp.concatenate(picks, axis=0)
        return carry

    lax.fori_loop(0, PEER_HEADS, head, 0)
    for lt in lane_tiles:
        tok = pl.ds(lt * LANES, LANES)
        e_t = e_ref[lt].T
        rows_ref[tok, :] = (e_t & (PACKED_ROWS - 1)) * VREG_SUBLANES
        shift_ref[tok, :] = (1 - (e_t >> 13)) * 16
        gate_ref[tok, :] = p_ref[lt].T


def _peer_route(h, g_ffn, w_q, sub_keys):
    T, d = h.shape
    tt = ROUTE_TOKEN_TILE
    assert T % tt == 0 and PACKED_ROWS == 1 << 13
    lts = tt // LANES
    sel_a, sel_b, pad = _stair_constants()
    keys = sub_keys.reshape(PEER_HEADS * 2, PEER_KEYS, PEER_HALF).astype(jnp.bfloat16)
    tok_spec = pl.BlockSpec((tt, d), lambda i: (i, 0))
    pick_spec = pl.BlockSpec((tt, PEER_PICKS), lambda i: (i, 0))

    def whole(shape):
        return pl.BlockSpec(shape, lambda i: (0,) * len(shape))

    return pl.pallas_call(
        _peer_route_body,
        grid=(T // tt,),
        in_specs=[tok_spec, whole((1, d)), whole((d, PEER_HEADS * PEER_QDIM)),
                  whole(keys.shape), whole(sel_a.shape), whole(sel_b.shape), whole(pad.shape)],
        out_specs=[tok_spec, pick_spec, pick_spec, pick_spec],
        out_shape=[jax.ShapeDtypeStruct((T, d), jnp.float32),
                   jax.ShapeDtypeStruct((T, PEER_PICKS), jnp.int32),
                   jax.ShapeDtypeStruct((T, PEER_PICKS), jnp.int32),
                   jax.ShapeDtypeStruct((T, PEER_PICKS), jnp.float32)],
        scratch_shapes=[pltpu.VMEM((2 * PEER_HEADS, tt, PEER_HALF), jnp.bfloat16),
                        pltpu.VMEM((lts, 2, PEER_TOPK, LANES), jnp.float32),
                        pltpu.VMEM((lts, 2, PEER_TOPK, LANES), jnp.float32),
                        pltpu.VMEM((lts, PEER_TOPK, LANES), jnp.float32),
                        pltpu.VMEM((lts, PEER_TOPK, LANES), jnp.float32),
                        pltpu.VMEM((lts, PEER_PICKS, LANES), jnp.int32),
                        pltpu.VMEM((lts, PEER_PICKS, LANES), jnp.float32)],
        compiler_params=pltpu.CompilerParams(dimension_semantics=("arbitrary",)),
        name="peer_route",
    )(h, g_ffn.reshape(1, d), w_q.astype(jnp.bfloat16), keys, sel_a, sel_b, pad)


ROW_SUBLANES = D_MODEL // LANES
PEER_TOKEN_TILE = 256
PACKED_TABLE_BYTES = PACKED_ROWS * D_MODEL * 4
BLOCK_TOKENS = 8
LANE_REPLICATED_BYTES = BLOCK_TOKENS * PEER_PICKS * LANES * 4
PEER_VMEM_LIMIT = PACKED_TABLE_BYTES + 4 * LANE_REPLICATED_BYTES + 8 * 1024 * 1024
HIGH_HALF_MASK = 0xFFFF0000
SHIFT_FIELD_MASK = 0x1F
_SLOT_PICK = (0, 2, 4, 6, 1, 3, 5, 7)


def _pack_expert_table(tab):
    bits = lax.bitcast_convert_type(tab.astype(jnp.bfloat16), jnp.uint16).astype(jnp.uint32)
    packed = bits[:PACKED_ROWS] | (bits[PACKED_ROWS:] << 16)
    return packed.reshape(PACKED_ROWS * ROW_SUBLANES, LANES)


def _gelu_exact(s):
    return 0.5 * s * (1.0 + lax.erf(s * (2.0 ** -0.5)))


OUT_PARTIAL_SUMS = 4


def _replicate(src_ref, t, dst_ref, slot):
    dst_ref[slot] = jnp.broadcast_to(src_ref[pl.ds(t, 1), :], (PEER_PICKS, LANES)).T


def _splat_row(ref, slot, k):
    return jnp.broadcast_to(ref[slot, pl.ds(k, 1), :], (ROW_SUBLANES, LANES))


def _expert_row(tab_ref, row8, shift):
    w = tab_ref[pl.ds(pl.multiple_of(row8, ROW_SUBLANES), ROW_SUBLANES), :]
    return lax.bitcast_convert_type((w << shift) & jnp.uint32(HIGH_HALF_MASK), jnp.float32)


def _pack_act(act, shift):
    bits = lax.bitcast_convert_type(act, jnp.uint32)
    rounded = bits + jnp.uint32(0x7FFF) + ((bits >> 16) & jnp.uint32(1))
    return (rounded & jnp.uint32(HIGH_HALF_MASK)) | shift.astype(jnp.uint32)


def _fold_half(a, b, shift, keep_a):
    return jnp.where(keep_a, a, b) + pltpu.roll(jnp.where(keep_a, b, a), shift, axis=0)


def _fold_pair(a, b, shift, keep_a):
    return (jnp.where(keep_a, a, pltpu.roll(b, shift, axis=0))
            + jnp.where(keep_a, pltpu.roll(a, ROW_SUBLANES - shift, axis=0), b))


def _sublane_sums(tiles, masks):
    keep2, keep4, keep1 = masks
    level = [tiles[_SLOT_PICK[s]] for s in range(ROW_SUBLANES)]
    level = [_fold_half(level[2 * i], level[2 * i + 1], 2, keep2) for i in range(4)]
    level = [_fold_half(level[2 * i], level[2 * i + 1], 4, keep4) for i in range(2)]
    return _fold_pair(level[0], level[1], 1, keep1)


def _two_buffer_token_loop(step):
    def block_pair(i, carry):
        base = i * (2 * BLOCK_TOKENS)
        for cur in range(2):
            def one(u, c, cur=cur):
                step(base + cur * BLOCK_TOKENS + u, u, cur, 1 - cur)
                return c

            lax.fori_loop(0, BLOCK_TOKENS, one, 0)
        return carry

    lax.fori_loop(0, PEER_TOKEN_TILE // (2 * BLOCK_TOKENS), block_pair, 0)


def _peer_hidden_body(row_ref, shift_ref, x_ref, gate_ref, utab_ref, act_ref,
                      sh0_ref, sh1_ref, part0_ref, part1_ref, s_ref):
    sub = lax.broadcasted_iota(jnp.int32, (ROW_SUBLANES, LANES), 0)
    masks = ((sub & 2) == 0, (sub & 4) == 0, (sub & 1) == 0)
    sh = (sh0_ref, sh1_ref)
    part = (part0_ref, part1_ref)
    last = PEER_TOKEN_TILE - 1

    def lane_sums(src_ref, slot, t):
        s_ref[pl.ds(t, 1), :] = jnp.sum(src_ref[slot].T, axis=0, keepdims=True)

    for u in range(BLOCK_TOKENS):
        _replicate(shift_ref, u, sh0_ref, u)
    part1_ref[...] = jnp.zeros(part1_ref.shape, jnp.float32)

    def step(t, slot, cur, nxt):
        x = x_ref[t]
        for g in range(PEER_PICKS // ROW_SUBLANES):
            picks = range(g * ROW_SUBLANES, (g + 1) * ROW_SUBLANES)
            prods = [_expert_row(utab_ref, row_ref[t, k],
                                 _splat_row(sh[cur], slot, k).astype(jnp.uint32)) * x
                     for k in picks]
            part[cur][slot, pl.ds(g * ROW_SUBLANES, ROW_SUBLANES), :] = _sublane_sums(prods, masks)
        _replicate(shift_ref, jnp.minimum(t + BLOCK_TOKENS, last), sh[nxt], slot)
        lane_sums(part[nxt], slot, jnp.maximum(t - BLOCK_TOKENS, 0))

    _two_buffer_token_loop(step)
    for u in range(BLOCK_TOKENS):
        lane_sums(part1_ref, u, PEER_TOKEN_TILE - BLOCK_TOKENS + u)
    act_ref[...] = _pack_act(_gelu_exact(s_ref[...]) * gate_ref[...], shift_ref[...])


def _peer_out_body(row_ref, act_ref, vtab_ref, y_ref, pk0_ref, pk1_ref):
    pk = (pk0_ref, pk1_ref)
    last = PEER_TOKEN_TILE - 1
    for u in range(BLOCK_TOKENS):
        _replicate(act_ref, u, pk0_ref, u)

    def step(t, slot, cur, nxt):
        accs = [jnp.zeros((ROW_SUBLANES, LANES), jnp.float32) for _ in range(OUT_PARTIAL_SUMS)]
        for k in range(PEER_PICKS):
            v = _splat_row(pk[cur], slot, k)
            a = lax.bitcast_convert_type(v & jnp.uint32(HIGH_HALF_MASK), jnp.float32)
            r = _expert_row(vtab_ref, row_ref[t, k], v & jnp.uint32(SHIFT_FIELD_MASK))
            accs[k % OUT_PARTIAL_SUMS] = accs[k % OUT_PARTIAL_SUMS] + a * r
        y_ref[t] = (accs[0] + accs[1]) + (accs[2] + accs[3])
        _replicate(act_ref, jnp.minimum(t + BLOCK_TOKENS, last), pk[nxt], slot)

    _two_buffer_token_loop(step)


def _peer_experts(xn, rows8, shifts, gates, expert_u, expert_v):
    T, d = xn.shape
    tt = PEER_TOKEN_TILE
    assert T % tt == 0 and d == ROW_SUBLANES * LANES and ROW_SUBLANES == VREG_SUBLANES
    assert tt % (2 * BLOCK_TOKENS) == 0
    x3 = xn.reshape(T, ROW_SUBLANES, LANES)
    smem_spec = pl.BlockSpec((tt, PEER_PICKS), lambda i: (i, 0), memory_space=pltpu.SMEM)
    pick_spec = pl.BlockSpec((tt, PEER_PICKS), lambda i: (i, 0))
    tok_spec = pl.BlockSpec((tt, ROW_SUBLANES, LANES), lambda i: (i, 0, 0))
    table_spec = pl.BlockSpec(memory_space=pltpu.VMEM)
    params = pltpu.CompilerParams(dimension_semantics=("arbitrary",),
                                  vmem_limit_bytes=PEER_VMEM_LIMIT)

    def replicated(dtype):
        return pltpu.VMEM((BLOCK_TOKENS, PEER_PICKS, LANES), dtype)

    act = pl.pallas_call(
        _peer_hidden_body,
        grid=(T // tt,),
        in_specs=[smem_spec, pick_spec, tok_spec, pick_spec, table_spec],
        out_specs=pick_spec,
        out_shape=jax.ShapeDtypeStruct((T, PEER_PICKS), jnp.uint32),
        scratch_shapes=[replicated(jnp.int32), replicated(jnp.int32),
                        replicated(jnp.float32), replicated(jnp.float32),
                        pltpu.VMEM((tt, PEER_PICKS), jnp.float32)],
        compiler_params=params,
        name="peer_hidden",
    )(rows8, shifts, x3, gates, _pack_expert_table(expert_u))
    y = pl.pallas_call(
        _peer_out_body,
        grid=(T // tt,),
        in_specs=[smem_spec, pick_spec, table_spec],
        out_specs=tok_spec,
        out_shape=jax.ShapeDtypeStruct((T, ROW_SUBLANES, LANES), jnp.float32),
        scratch_shapes=[replicated(jnp.uint32), replicated(jnp.uint32)],
        compiler_params=params,
        name="peer_out",
    )(rows8, act, _pack_expert_table(expert_v))
    return y.reshape(T, d)


def _peer_ffn(h, g_ffn, w_q, sub_keys, expert_u, expert_v):
    xn, rows8, shifts, gates = _peer_route(h, g_ffn, w_q, sub_keys)
    return _peer_experts(xn, rows8, shifts, gates, expert_u, expert_v)


def _final_norm_body(h_ref, f_ref, g_ref, o_ref):
    xf = h_ref[...] + f_ref[...]
    ms = jnp.mean(xf * xf, axis=-1, keepdims=True)
    o_ref[...] = xf * lax.rsqrt(ms + EPS) * g_ref[...]


def _final_norm(h, ffn, g):
    rows, d = h.shape
    tile = 1024
    return pl.pallas_call(
        _final_norm_body,
        grid=(rows // tile,),
        in_specs=[pl.BlockSpec((tile, d), lambda i: (i, 0)),
                  pl.BlockSpec((tile, d), lambda i: (i, 0)),
                  pl.BlockSpec((1, d), lambda i: (0, 0))],
        out_specs=pl.BlockSpec((tile, d), lambda i: (i, 0)),
        out_shape=jax.ShapeDtypeStruct((rows, d), jnp.float32),
        name="final_norm",
    )(h, ffn, g.reshape(1, d))


def kernel(x, meta_tokens, g_mix, w_in, conv_ssd_w, conv_ssd_b, dt_bias, a_log, d_skip,
           ssd_norm_w, conv_sc_w, w_out, g_ffn, w_q, sub_keys, expert_u, expert_v, g_final):
    bsz, seq, d = x.shape
    assert g_mix.shape[0] == 1
    h = _mixer(x, meta_tokens, g_mix[0], w_in[0], conv_ssd_w[0], conv_ssd_b[0], dt_bias[0],
               a_log[0], d_skip[0], ssd_norm_w[0], conv_sc_w[0], w_out[0])
    ffn = _peer_ffn(h, g_ffn[0], w_q[0], sub_keys[0], expert_u[0], expert_v[0])
    return _final_norm(h, ffn, g_final).reshape(bsz, seq, d)
```

```python
import numpy as np

import jax
import jax.numpy as jnp
from jax import lax
from jax.experimental import pallas as pl
from jax.experimental.pallas import tpu as pltpu

D_MODEL = 1024
N_META = 16
D_MIX = 2 * D_MODEL
SSD_HEAD_DIM = 64
SSD_HEADS = 24
D_SSD = SSD_HEADS * SSD_HEAD_DIM
SSD_GROUPS = 4
SSD_STATE = 128
SSD_CONV = 4
SSD_CHUNK = 128
D_XBC = D_SSD + 2 * SSD_GROUPS * SSD_STATE
D_SC = D_MIX - D_SSD
SC_CONV = 3
D_IN = D_SSD + D_XBC + SSD_HEADS + 3 * D_SC
PEER_HEADS = 8
PEER_KEYS = 128
PEER_TOPK = 16
PEER_QDIM = 256
PEER_HALF = PEER_QDIM // 2
EPS = 1e-6

LANES = 128
VREG_SUBLANES = 8


SSD_PAD = SSD_CHUNK - N_META
HEADS_PER_GROUP = SSD_HEADS // SSD_GROUPS
D_GROUP = HEADS_PER_GROUP * SSD_HEAD_DIM
D_BC = SSD_GROUPS * SSD_STATE
D_SC3 = 3 * D_SC
D_IN_ALIGNED = D_SSD + D_XBC + LANES + D_SC3
INPROJ_TOKEN_TILE = 256
INPROJ_COL_TILE = 512
HALO_ROWS = VREG_SUBLANES


def _inproj_body(h_ref, g_ref, w_ref, z_ref, xbc_ref, dt_ref, sc_ref):
    hx = h_ref[...]
    xn = hx * lax.rsqrt(jnp.mean(hx * hx, axis=-1, keepdims=True) + EPS) * g_ref[...]
    xb = xn.astype(jnp.bfloat16)
    col = 0
    for out_ref in (z_ref, xbc_ref, dt_ref, sc_ref):
        width = out_ref.shape[1]
        for c0 in range(0, width, INPROJ_COL_TILE):
            cw = min(INPROJ_COL_TILE, width - c0)
            out_ref[:, c0:c0 + cw] = jnp.dot(xb, w_ref[:, col + c0:col + c0 + cw],
                                             preferred_element_type=jnp.float32)
        col += width


def _inproj(h_pad, g_mix, w_in):
    rows, d = h_pad.shape
    tm = INPROJ_TOKEN_TILE
    assert rows % tm == 0
    s1, s2, s3 = D_SSD, D_SSD + D_XBC, D_SSD + D_XBC + SSD_HEADS
    w = jnp.concatenate([w_in[:, :s2], w_in[:, s2:s3],
                         jnp.zeros((d, LANES - SSD_HEADS), w_in.dtype), w_in[:, s3:]],
                        axis=1).astype(jnp.bfloat16)
    widths = (D_SSD, D_XBC, LANES, D_SC3)
    return pl.pallas_call(
        _inproj_body,
        grid=(rows // tm,),
        in_specs=[pl.BlockSpec((tm, d), lambda i: (i, 0)),
                  pl.BlockSpec((1, d), lambda i: (0, 0)),
                  pl.BlockSpec((d, D_IN_ALIGNED), lambda i: (0, 0))],
        out_specs=[pl.BlockSpec((tm, wd), lambda i: (i, 0)) for wd in widths],
        out_shape=[jax.ShapeDtypeStruct((rows, wd), jnp.float32) for wd in widths],
        compiler_params=pltpu.CompilerParams(dimension_semantics=("arbitrary",)),
        name="mixer_inproj",
    )(h_pad, g_mix.reshape(1, d), w)


def _silu(v):
    return v / (1.0 + jnp.exp(-v))


def _mixer_chunk_body(z_ref, xbc_ref, dt_ref, sc_ref, x_ref, cw_ref, cb_ref, dtb_ref, alog_ref,
                      dskip_ref, nw_ref, scw_ref, expand_ref, wout_ref, o_ref,
                      ext_ref, ext2_ref, state_ref):
    c = pl.program_id(1)
    q = SSD_CHUNK
    hi = lax.Precision.HIGHEST

    @pl.when(c == 0)
    def _():
        ext_ref[0:HALO_ROWS, :] = jnp.zeros((HALO_ROWS, D_XBC), jnp.float32)
        ext2_ref[0:HALO_ROWS, :] = jnp.zeros((HALO_ROWS, D_SC), jnp.float32)
        state_ref[...] = jnp.zeros(state_ref.shape, jnp.float32)

    row = lax.broadcasted_iota(jnp.int32, (q, 1), 0)
    live = jnp.logical_or(c > 0, row >= SSD_PAD)

    ext_ref[HALO_ROWS:HALO_ROWS + q, :] = xbc_ref[...]
    conv = jnp.broadcast_to(cb_ref[...], (q, D_XBC))
    for j in range(SSD_CONV):
        conv = conv + cw_ref[j:j + 1, :] * ext_ref[pl.ds(HALO_ROWS - (SSD_CONV - 1) + j, q), :]
    ext_ref[0:HALO_ROWS, :] = ext_ref[q:q + HALO_ROWS, :]
    xbc = jnp.where(live, _silu(conv), 0.0)
    xs = xbc[:, :D_SSD]
    bm = xbc[:, D_SSD:D_SSD + D_BC]
    cm = xbc[:, D_SSD + D_BC:]

    dtr = dt_ref[...] + dtb_ref[...]
    dt = jnp.maximum(dtr, 0.0) + jnp.log1p(jnp.exp(-jnp.abs(dtr)))
    dt = jnp.where(live, dt, 0.0)
    a = dt * (-jnp.exp(alog_ref[...]))
    tri = (lax.broadcasted_iota(jnp.int32, (q, q), 0)
           >= lax.broadcasted_iota(jnp.int32, (q, q), 1))
    a_cs = jnp.dot(tri.astype(jnp.float32), a, precision=hi,
                   preferred_element_type=jnp.float32)
    a_cs_t = a_cs.T
    a_last = a_cs[q - 1:q, :]
    expand = expand_ref[...]
    dt_w = jnp.dot(dt, expand, precision=hi, preferred_element_type=jnp.float32)
    acs_w = jnp.dot(a_cs, expand, precision=hi, preferred_element_type=jnp.float32)
    alast_w = jnp.dot(a_last, expand, precision=hi, preferred_element_type=jnp.float32)
    x_dt = xs * dt_w
    grow = jnp.exp(acs_w)
    x_tail = (x_dt * jnp.exp(alast_w - acs_w)).astype(jnp.bfloat16)
    x_dt_b = x_dt.astype(jnp.bfloat16)
    lane = lax.broadcasted_iota(jnp.int32, (q, LANES), 1)

    y_groups = []
    for g in range(SSD_GROUPS):
        gs = slice(g * D_GROUP, (g + 1) * D_GROUP)
        bg = bm[:, g * SSD_STATE:(g + 1) * SSD_STATE]
        cg = cm[:, g * SSD_STATE:(g + 1) * SSD_STATE].astype(jnp.bfloat16)
        cb = lax.dot_general(cg, bg.astype(jnp.bfloat16), (((1,), (1,)), ((), ())),
                             preferred_element_type=jnp.float32)
        state = state_ref[g]
        y_off = jnp.dot(cg, state.astype(jnp.bfloat16),
                        preferred_element_type=jnp.float32) * grow[:, gs]
        y_pairs = []
        for pair in range(HEADS_PER_GROUP // 2):
            cols = slice(g * D_GROUP + pair * LANES, g * D_GROUP + (pair + 1) * LANES)
            xp = x_dt_b[:, cols]
            halves = []
            for sub in range(2):
                hd = g * HEADS_PER_GROUP + 2 * pair + sub
                seg = (jnp.broadcast_to(a_cs[:, hd:hd + 1], (q, q))
                       - jnp.broadcast_to(a_cs_t[hd:hd + 1, :], (q, q)))
                decay = jnp.exp(jnp.where(tri, seg, -jnp.inf))
                halves.append(jnp.dot((cb * decay).astype(jnp.bfloat16), xp,
                                      preferred_element_type=jnp.float32))
            y_pairs.append(jnp.where(lane < SSD_HEAD_DIM, halves[0], halves[1]))
        y_groups.append(jnp.concatenate(y_pairs, axis=1) + y_off)
        bg_t = bg.T.astype(jnp.bfloat16)
        state_ref[g] = (state * jnp.exp(alast_w[:, gs])
                        + jnp.dot(bg_t, x_tail[:, gs], preferred_element_type=jnp.float32))

    y = jnp.concatenate(y_groups, axis=1) + dskip_ref[...] * xs
    y = y * _silu(z_ref[...])
    normed = []
    for g in range(SSD_GROUPS):
        yg = y[:, g * D_GROUP:(g + 1) * D_GROUP]
        ms = jnp.mean(yg * yg, axis=-1, keepdims=True)
        normed.append(yg * lax.rsqrt(ms + EPS))
    y_ssd = jnp.concatenate(normed, axis=1) * nw_ref[...]

    sc = sc_ref[...]
    ext2_ref[HALO_ROWS:HALO_ROWS + q, :] = sc[:, D_SC:2 * D_SC] * sc[:, 2 * D_SC:]
    conv2 = jnp.zeros((q, D_SC), jnp.float32)
    for j in range(SC_CONV):
        conv2 = conv2 + scw_ref[j:j + 1, :] * ext2_ref[pl.ds(HALO_ROWS - (SC_CONV - 1) + j, q), :]
    ext2_ref[0:HALO_ROWS, :] = ext2_ref[q:q + HALO_ROWS, :]
    y_sc = sc[:, :D_SC] * conv2

    @pl.when(c > 0)
    def _():
        ycat = jnp.concatenate([y_ssd, y_sc], axis=1).astype(jnp.bfloat16)
        o_ref[...] = x_ref[...] + jnp.dot(ycat, wout_ref[...],
                                          preferred_element_type=jnp.float32)


def _head_expand_matrix():
    e = np.zeros((LANES, D_SSD), np.float32)
    for hd in range(SSD_HEADS):
        e[hd, hd * SSD_HEAD_DIM:(hd + 1) * SSD_HEAD_DIM] = 1.0
    return jnp.asarray(e)


def _mixer(x, meta_tokens, g_mix, w_in, conv_ssd_w, conv_ssd_b, dt_bias, a_log, d_skip,
           ssd_norm_w, conv_sc_w, w_out):
    bsz, seq, d = x.shape
    assert seq % SSD_CHUNK == 0
    n_chunks = seq // SSD_CHUNK + 1
    meta = jnp.broadcast_to(meta_tokens[None].astype(x.dtype), (bsz, N_META, d))
    h_pad = jnp.concatenate([jnp.zeros((bsz, SSD_PAD, d), x.dtype), meta, x], axis=1)
    z, xbc, dt_raw, sc = _inproj(h_pad.reshape(bsz * n_chunks * SSD_CHUNK, d), g_mix, w_in)

    def pad_heads(v):
        return jnp.pad(v.astype(jnp.float32), (0, LANES - SSD_HEADS)).reshape(1, LANES)

    params = [conv_ssd_w, conv_ssd_b.reshape(1, D_XBC), pad_heads(dt_bias), pad_heads(a_log),
              jnp.repeat(d_skip.astype(jnp.float32), SSD_HEAD_DIM).reshape(1, D_SSD),
              ssd_norm_w.reshape(1, D_SSD), conv_sc_w, _head_expand_matrix(),
              w_out.astype(jnp.bfloat16)]

    def chunk_rows(width):
        return pl.BlockSpec((SSD_CHUNK, width), lambda b, c: (b * n_chunks + c, 0))

    def seq_rows(b, c):
        return (b * (n_chunks - 1) + jnp.maximum(c - 1, 0), 0)

    def whole(arr):
        return pl.BlockSpec(arr.shape, lambda b, c: (0,) * arr.ndim)

    return pl.pallas_call(
        _mixer_chunk_body,
        grid=(bsz, n_chunks),
        in_specs=[chunk_rows(D_SSD), chunk_rows(D_XBC), chunk_rows(LANES), chunk_rows(D_SC3),
                  pl.BlockSpec((SSD_CHUNK, d), seq_rows)] + [whole(p) for p in params],
        out_specs=pl.BlockSpec((SSD_CHUNK, d), seq_rows),
        out_shape=jax.ShapeDtypeStruct((bsz * seq, d), jnp.float32),
        scratch_shapes=[pltpu.VMEM((HALO_ROWS + SSD_CHUNK, D_XBC), jnp.float32),
                        pltpu.VMEM((HALO_ROWS + SSD_CHUNK, D_SC), jnp.float32),
                        pltpu.VMEM((SSD_GROUPS, SSD_STATE, D_GROUP), jnp.float32)],
        compiler_params=pltpu.CompilerParams(dimension_semantics=("arbitrary", "arbitrary")),
        name="mixer_chunk",
    )(z, xbc, dt_raw, sc, x.reshape(bsz * seq, d), *params)


PEER_PICKS = PEER_HEADS * PEER_TOPK
PACKED_ROWS = PEER_KEYS * PEER_KEYS // 2
ROUTE_TOKEN_TILE = 512

_STAIR = [(a, b) for a in range(PEER_TOPK) for b in range(PEER_TOPK)
          if (a + 1) * (b + 1) <= PEER_TOPK]
STAIR_ROWS = -(-len(_STAIR) // VREG_SUBLANES) * VREG_SUBLANES


def _stair_constants():
    sel_a = np.zeros((STAIR_ROWS, PEER_TOPK), np.float32)
    sel_b = np.zeros((STAIR_ROWS, PEER_TOPK), np.float32)
    pad = np.full((STAIR_ROWS, LANES), -np.inf, np.float32)
    for c, (a, b) in enumerate(_STAIR):
        sel_a[c, a] = 1.0
        sel_b[c, b] = 1.0
        pad[c, :] = 0.0
    return jnp.asarray(sel_a), jnp.asarray(sel_b), jnp.asarray(pad)


def _top16_rows(chains, val_refs, idx_refs):
    rows = chains[0].shape[0]
    iota = lax.broadcasted_iota(jnp.int32, chains[0].shape, 0).astype(jnp.float32)
    chains = list(chains)
    for r in range(PEER_TOPK):
        for n, c in enumerate(chains):
            m = jnp.max(c, axis=0, keepdims=True)
            idx = jnp.min(jnp.where(c == m, iota, float(rows)), axis=0, keepdims=True)
            val_refs[n][pl.ds(r, 1), :] = m
            idx_refs[n][pl.ds(r, 1), :] = idx
            chains[n] = jnp.where(iota == idx, -jnp.inf, c)


def _peer_route_body(h_ref, g_ref, wq_ref, keys_ref, sela_ref, selb_ref, pad_ref,
                     xn_ref, rows_ref, shift_ref, gate_ref,
                     q_ref, v_ref, i_ref, best_ref, pos_ref, e_ref, p_ref):
    hx = h_ref[...]
    xn = hx * lax.rsqrt(jnp.mean(hx * hx, axis=-1, keepdims=True) + EPS) * g_ref[...]
    xn_ref[...] = xn
    xb = xn.astype(jnp.bfloat16)
    for hh in range(PEER_HEADS):
        qh = jnp.dot(xb, wq_ref[:, hh * PEER_QDIM:(hh + 1) * PEER_QDIM],
                     preferred_element_type=jnp.float32).astype(jnp.bfloat16)
        q_ref[2 * hh] = qh[:, :PEER_HALF]
        q_ref[2 * hh + 1] = qh[:, PEER_HALF:]

    lane_tiles = range(ROUTE_TOKEN_TILE // LANES)
    hi = lax.Precision.HIGHEST

    def head(hh, carry):
        for lt in lane_tiles:
            scores = [lax.dot_general(keys_ref[2 * hh + half],
                                      q_ref[2 * hh + half, pl.ds(lt * LANES, LANES), :],
                                      (((1,), (1,)), ((), ())),
                                      preferred_element_type=jnp.float32)
                      for half in range(2)]
            _top16_rows(scores, [v_ref.at[lt, 0], v_ref.at[lt, 1]],
                        [i_ref.at[lt, 0], i_ref.at[lt, 1]])
        cands, cand_es = [], []
        for lt in lane_tiles:
            cands.append(
                jnp.dot(sela_ref[...], v_ref[lt, 0], precision=hi,
                        preferred_element_type=jnp.float32)
                + jnp.dot(selb_ref[...], v_ref[lt, 1], precision=hi,
                          preferred_element_type=jnp.float32)
                + pad_ref[...])
            ea = jnp.dot(sela_ref[...], i_ref[lt, 0], preferred_element_type=jnp.float32)
            eb = jnp.dot(selb_ref[...], i_ref[lt, 1], preferred_element_type=jnp.float32)
            cand_es.append((ea * PEER_KEYS + eb).astype(jnp.int32))
        _top16_rows(cands, [best_ref.at[lt] for lt in lane_tiles],
                    [pos_ref.at[lt] for lt in lane_tiles])
        crow = lax.broadcasted_iota(jnp.int32, cands[0].shape, 0).astype(jnp.float32)
        base = pl.multiple_of(hh * PEER_TOPK, PEER_TOPK)
        for lt in lane_tiles:
            best = best_ref[lt]
            pos = pos_ref[lt]
            picks = [jnp.sum(jnp.where(crow == pos[r:r + 1, :], cand_es[lt], 0),
                             axis=0, keepdims=True)
                     for r in range(PEER_TOPK)]
            w = jnp.exp(best - best[0:1, :])
            p_ref[lt, pl.ds(base, PEER_TOPK), :] = w / jnp.sum(w, axis=0, keepdims=True)
            e_ref[lt, pl.ds(base, PEER_TOPK), :] = jnp.concatenate(picks, axis=0)
        return carry

    lax.fori_loop(0, PEER_HEADS, head, 0)
    for lt in lane_tiles:
        tok = pl.ds(lt * LANES, LANES)
        e_t = e_ref[lt].T
        rows_ref[tok, :] = (e_t & (PACKED_ROWS - 1)) * VREG_SUBLANES
        shift_ref[tok, :] = (1 - (e_t >> 13)) * 16
        gate_ref[tok, :] = p_ref[lt].T


def _peer_route(h, g_ffn, w_q, sub_keys):
    T, d = h.shape
    tt = ROUTE_TOKEN_TILE
    assert T % tt == 0 and PACKED_ROWS == 1 << 13
    lts = tt // LANES
    sel_a, sel_b, pad = _stair_constants()
    keys = sub_keys.reshape(PEER_HEADS * 2, PEER_KEYS, PEER_HALF).astype(jnp.bfloat16)
    tok_spec = pl.BlockSpec((tt, d), lambda i: (i, 0))
    pick_spec = pl.BlockSpec((tt, PEER_PICKS), lambda i: (i, 0))

    def whole(shape):
        return pl.BlockSpec(shape, lambda i: (0,) * len(shape))

    return pl.pallas_call(
        _peer_route_body,
        grid=(T // tt,),
        in_specs=[tok_spec, whole((1, d)), whole((d, PEER_HEADS * PEER_QDIM)),
                  whole(keys.shape), whole(sel_a.shape), whole(sel_b.shape), whole(pad.shape)],
        out_specs=[tok_spec, pick_spec, pick_spec, pick_spec],
        out_shape=[jax.ShapeDtypeStruct((T, d), jnp.float32),
                   jax.ShapeDtypeStruct((T, PEER_PICKS), jnp.int32),
                   jax.ShapeDtypeStruct((T, PEER_PICKS), jnp.int32),
                   jax.ShapeDtypeStruct((T, PEER_PICKS), jnp.float32)],
        scratch_shapes=[pltpu.VMEM((2 * PEER_HEADS, tt, PEER_HALF), jnp.bfloat16),
                        pltpu.VMEM((lts, 2, PEER_TOPK, LANES), jnp.float32),
                        pltpu.VMEM((lts, 2, PEER_TOPK, LANES), jnp.float32),
                        pltpu.VMEM((lts, PEER_TOPK, LANES), jnp.float32),
                        pltpu.VMEM((lts, PEER_TOPK, LANES), jnp.float32),
                        pltpu.VMEM((lts, PEER_PICKS, LANES), jnp.int32),
                        pltpu.VMEM((lts, PEER_PICKS, LANES), jnp.float32)],
        compiler_params=pltpu.CompilerParams(dimension_semantics=("arbitrary",)),
        name="peer_route",
    )(h, g_ffn.reshape(1, d), w_q.astype(jnp.bfloat16), keys, sel_a, sel_b, pad)


ROW_SUBLANES = D_MODEL // LANES
PEER_TOKEN_TILE = 256
PACKED_TABLE_BYTES = PACKED_ROWS * D_MODEL * 4
BLOCK_TOKENS = 8
LANE_REPLICATED_BYTES = BLOCK_TOKENS * PEER_PICKS * LANES * 4
PEER_VMEM_LIMIT = PACKED_TABLE_BYTES + 4 * LANE_REPLICATED_BYTES + 8 * 1024 * 1024
HIGH_HALF_MASK = 0xFFFF0000
SHIFT_FIELD_MASK = 0x1F
_SLOT_PICK = (0, 2, 4, 6, 1, 3, 5, 7)


def _pack_expert_table(tab):
    bits = lax.bitcast_convert_type(tab.astype(jnp.bfloat16), jnp.uint16).astype(jnp.uint32)
    packed = bits[:PACKED_ROWS] | (bits[PACKED_ROWS:] << 16)
    return packed.reshape(PACKED_ROWS * ROW_SUBLANES, LANES)


def _gelu_exact(s):
    return 0.5 * s * (1.0 + lax.erf(s * (2.0 ** -0.5)))


OUT_PARTIAL_SUMS = 4


def _replicate(src_ref, t, dst_ref, slot):
    dst_ref[slot] = jnp.broadcast_to(src_ref[pl.ds(t, 1), :], (PEER_PICKS, LANES)).T


def _splat_row(ref, slot, k):
    return jnp.broadcast_to(ref[slot, pl.ds(k, 1), :], (ROW_SUBLANES, LANES))


def _expert_row(tab_ref, row8, shift):
    w = tab_ref[pl.ds(pl.multiple_of(row8, ROW_SUBLANES), ROW_SUBLANES), :]
    return lax.bitcast_convert_type((w << shift) & jnp.uint32(HIGH_HALF_MASK), jnp.float32)


def _pack_act(act, shift):
    bits = lax.bitcast_convert_type(act, jnp.uint32)
    rounded = bits + jnp.uint32(0x7FFF) + ((bits >> 16) & jnp.uint32(1))
    return (rounded & jnp.uint32(HIGH_HALF_MASK)) | shift.astype(jnp.uint32)


def _fold_half(a, b, shift, keep_a):
    return jnp.where(keep_a, a, b) + pltpu.roll(jnp.where(keep_a, b, a), shift, axis=0)


def _fold_pair(a, b, shift, keep_a):
    return (jnp.where(keep_a, a, pltpu.roll(b, shift, axis=0))
            + jnp.where(keep_a, pltpu.roll(a, ROW_SUBLANES - shift, axis=0), b))


def _sublane_sums(tiles, masks):
    keep2, keep4, keep1 = masks
    level = [tiles[_SLOT_PICK[s]] for s in range(ROW_SUBLANES)]
    level = [_fold_half(level[2 * i], level[2 * i + 1], 2, keep2) for i in range(4)]
    level = [_fold_half(level[2 * i], level[2 * i + 1], 4, keep4) for i in range(2)]
    return _fold_pair(level[0], level[1], 1, keep1)


def _two_buffer_token_loop(step):
    def block_pair(i, carry):
        base = i * (2 * BLOCK_TOKENS)
        for cur in range(2):
            def one(u, c, cur=cur):
                step(base + cur * BLOCK_TOKENS + u, u, cur, 1 - cur)
                return c

            lax.fori_loop(0, BLOCK_TOKENS, one, 0)
        return carry

    lax.fori_loop(0, PEER_TOKEN_TILE // (2 * BLOCK_TOKENS), block_pair, 0)


def _peer_hidden_body(row_ref, shift_ref, x_ref, gate_ref, utab_ref, act_ref,
                      sh0_ref, sh1_ref, part0_ref, part1_ref, s_ref):
    sub = lax.broadcasted_iota(jnp.int32, (ROW_SUBLANES, LANES), 0)
    masks = ((sub & 2) == 0, (sub & 4) == 0, (sub & 1) == 0)
    sh = (sh0_ref, sh1_ref)
    part = (part0_ref, part1_ref)
    last = PEER_TOKEN_TILE - 1

    def lane_sums(src_ref, slot, t):
        s_ref[pl.ds(t, 1), :] = jnp.sum(src_ref[slot].T, axis=0, keepdims=True)

    for u in range(BLOCK_TOKENS):
        _replicate(shift_ref, u, sh0_ref, u)
    part1_ref[...] = jnp.zeros(part1_ref.shape, jnp.float32)

    def step(t, slot, cur, nxt):
        x = x_ref[t]
        for g in range(PEER_PICKS // ROW_SUBLANES):
            picks = range(g * ROW_SUBLANES, (g + 1) * ROW_SUBLANES)
            prods = [_expert_row(utab_ref, row_ref[t, k],
                                 _splat_row(sh[cur], slot, k).astype(jnp.uint32)) * x
                     for k in picks]
            part[cur][slot, pl.ds(g * ROW_SUBLANES, ROW_SUBLANES), :] = _sublane_sums(prods, masks)
        _replicate(shift_ref, jnp.minimum(t + BLOCK_TOKENS, last), sh[nxt], slot)
        lane_sums(part[nxt], slot, jnp.maximum(t - BLOCK_TOKENS, 0))

    _two_buffer_token_loop(step)
    for u in range(BLOCK_TOKENS):
        lane_sums(part1_ref, u, PEER_TOKEN_TILE - BLOCK_TOKENS + u)
    act_ref[...] = _pack_act(_gelu_exact(s_ref[...]) * gate_ref[...], shift_ref[...])


def _peer_out_body(row_ref, act_ref, vtab_ref, y_ref, pk0_ref, pk1_ref):
    pk = (pk0_ref, pk1_ref)
    last = PEER_TOKEN_TILE - 1
    for u in range(BLOCK_TOKENS):
        _replicate(act_ref, u, pk0_ref, u)

    def step(t, slot, cur, nxt):
        accs = [jnp.zeros((ROW_SUBLANES, LANES), jnp.float32) for _ in range(OUT_PARTIAL_SUMS)]
        for k in range(PEER_PICKS):
            v = _splat_row(pk[cur], slot, k)
            a = lax.bitcast_convert_type(v & jnp.uint32(HIGH_HALF_MASK), jnp.float32)
            r = _expert_row(vtab_ref, row_ref[t, k], v & jnp.uint32(SHIFT_FIELD_MASK))
            accs[k % OUT_PARTIAL_SUMS] = accs[k % OUT_PARTIAL_SUMS] + a * r
        y_ref[t] = (accs[0] + accs[1]) + (accs[2] + accs[3])
        _replicate(act_ref, jnp.minimum(t + BLOCK_TOKENS, last), pk[nxt], slot)

    _two_buffer_token_loop(step)


def _peer_experts(xn, rows8, shifts, gates, expert_u, expert_v):
    T, d = xn.shape
    tt = PEER_TOKEN_TILE
    assert T % tt == 0 and d == ROW_SUBLANES * LANES and ROW_SUBLANES == VREG_SUBLANES
    assert tt % (2 * BLOCK_TOKENS) == 0
    x3 = xn.reshape(T, ROW_SUBLANES, LANES)
    smem_spec = pl.BlockSpec((tt, PEER_PICKS), lambda i: (i, 0), memory_space=pltpu.SMEM)
    pick_spec = pl.BlockSpec((tt, PEER_PICKS), lambda i: (i, 0))
    tok_spec = pl.BlockSpec((tt, ROW_SUBLANES, LANES), lambda i: (i, 0, 0))
    table_spec = pl.BlockSpec(memory_space=pltpu.VMEM)
    params = pltpu.CompilerParams(dimension_semantics=("arbitrary",),
                                  vmem_limit_bytes=PEER_VMEM_LIMIT)

    def replicated(dtype):
        return pltpu.VMEM((BLOCK_TOKENS, PEER_PICKS, LANES), dtype)

    act = pl.pallas_call(
        _peer_hidden_body,
        grid=(T // tt,),
        in_specs=[smem_spec, pick_spec, tok_spec, pick_spec, table_spec],
        out_specs=pick_spec,
        out_shape=jax.ShapeDtypeStruct((T, PEER_PICKS), jnp.uint32),
        scratch_shapes=[replicated(jnp.int32), replicated(jnp.int32),
                        replicated(jnp.float32), replicated(jnp.float32),
                        pltpu.VMEM((tt, PEER_PICKS), jnp.float32)],
        compiler_params=params,
        name="peer_hidden",
    )(rows8, shifts, x3, gates, _pack_expert_table(expert_u))
    y = pl.pallas_call(
        _peer_out_body,
        grid=(T // tt,),
        in_specs=[smem_spec, pick_spec, table_spec],
        out_specs=tok_spec,
        out_shape=jax.ShapeDtypeStruct((T, ROW_SUBLANES, LANES), jnp.float32),
        scratch_shapes=[replicated(jnp.uint32), replicated(jnp.uint32)],
        compiler_params=params,
        name="peer_out",
    )(rows8, act, _pack_expert_table(expert_v))
    return y.reshape(T, d)


def _peer_ffn(h, g_ffn, w_q, sub_keys, expert_u, expert_v):
    xn, rows8, shifts, gates = _peer_route(h, g_ffn, w_q, sub_keys)
    return _peer_experts(xn, rows8, shifts, gates, expert_u, expert_v)


def _final_norm_body(h_ref, f_ref, g_ref, o_ref):
    xf = h_ref[...] + f_ref[...]
    ms = jnp.mean(xf * xf, axis=-1, keepdims=True)
    o_ref[...] = xf * lax.rsqrt(ms + EPS) * g_ref[...]


def _final_norm(h, ffn, g):
    rows, d = h.shape
    tile = 1024
    return pl.pallas_call(
        _final_norm_body,
        grid=(rows // tile,),
        in_specs=[pl.BlockSpec((tile, d), lambda i: (i, 0)),
                  pl.BlockSpec((tile, d), lambda i: (i, 0)),
                  pl.BlockSpec((1, d), lambda i: (0, 0))],
        out_specs=pl.BlockSpec((tile, d), lambda i: (i, 0)),
        out_shape=jax.ShapeDtypeStruct((rows, d), jnp.float32),
        name="final_norm",
    )(h, ffn, g.reshape(1, d))


def kernel(x, meta_tokens, g_mix, w_in, conv_ssd_w, conv_ssd_b, dt_bias, a_log, d_skip,
           ssd_norm_w, conv_sc_w, w_out, g_ffn, w_q, sub_keys, expert_u, expert_v, g_final):
    bsz, seq, d = x.shape
    assert g_mix.shape[0] == 1
    h = _mixer(x, meta_tokens, g_mix[0], w_in[0], conv_ssd_w[0], conv_ssd_b[0], dt_bias[0],
               a_log[0], d_skip[0], ssd_norm_w[0], conv_sc_w[0], w_out[0])
    ffn = _peer_ffn(h, g_ffn[0], w_q[0], sub_keys[0], expert_u[0], expert_v[0])
    return _final_norm(h, ffn, g_final).reshape(bsz, seq, d)
```
